```python
import jax
import jax.numpy as jnp
from jax import lax
import numpy as np

D_MODEL = 1024
BATCH = 16
SEQ = 4096
DEPTH = 2
DEC_BATCH = 16
DEC_SEQ = 16
PAST_LEN = 4096

CHUNK = 64
BRANCH_WIDTH = D_MODEL // 2
H_A = 4
DV_A = BRANCH_WIDTH // H_A
DK_A = DV_A // 2
GLA_RANK = 16
GLA_TAU = 16.0
H_B = 4
DK_B = BRANCH_WIDTH // H_B
DV_B = BRANCH_WIDTH // H_B
H_C = 4
DK_C = BRANCH_WIDTH // H_C
DV_C = BRANCH_WIDTH // H_C
CONV_C = 4
N_BRANCH = 3
D_FF = ((8 * D_MODEL // 3 + 127) // 128) * 128
CONV_F = 3
EPS = 1e-6

SPLIT_SIZES = (
    H_A * DK_A, H_A * DK_A, H_A * DV_A, H_A * DV_A, GLA_RANK,
    H_B * DK_B, H_B * DK_B, H_B * DV_B, H_B * DV_B,
    2 * H_C * DK_C, H_C * DV_C, H_C * DV_C, H_C, H_C,
    N_BRANCH * D_MODEL,
)
SPLIT_IDX = tuple(int(i) for i in np.cumsum(SPLIT_SIZES)[:-1])
N_IN = int(sum(SPLIT_SIZES))

kernel_name = 'hybrid_gla_hgrn2_mlstm_stream'


def _rmsnorm(x, g):
    xf = x.astype(jnp.float32)
    y = xf * lax.rsqrt(jnp.mean(xf * xf, axis=-1, keepdims=True) + EPS)
    return (y * g.astype(jnp.float32)).astype(x.dtype)


def _head_rmsnorm(o, g):
    B, T, H, V = o.shape
    o = o * lax.rsqrt(jnp.mean(o * o, axis=-1, keepdims=True) + EPS)
    return o.reshape(B, T, H * V) * g.astype(jnp.float32)


def _chunk_len(T):
    return CHUNK if T % CHUNK == 0 else T


def _to_chunks(a, L):
    B, T = a.shape[:2]
    return jnp.swapaxes(a.reshape(B, T // L, L, *a.shape[2:]), 0, 1)


def _from_chunks(a):
    n, B, L = a.shape[:3]
    return jnp.swapaxes(a, 0, 1).reshape(B, n * L, *a.shape[3:])


def _causal_dwconv(x, buf, w):
    xp = jnp.concatenate([buf.astype(x.dtype), x], axis=1)
    y = lax.conv_general_dilated(xp, w[:, None, :].astype(x.dtype), window_strides=(1,), padding='VALID',
                                 dimension_numbers=('NWC', 'WIO', 'NWC'), feature_group_count=x.shape[-1])
    return y, xp[:, -(w.shape[0] - 1):]


def _gated_linear_scan(q, k, v, log_a, S0):
    f32 = jnp.float32
    T = q.shape[1]
    L = _chunk_len(T)
    mask = jnp.tril(jnp.ones((L, L), dtype=bool))[None, :, :, None, None]

    def step(S, inp):
        qc, kc, vc, lac = inp
        b = jnp.cumsum(lac, axis=1)
        o_inter = jnp.einsum('bthk,bhkv->bthv', qc * jnp.exp(b), S)
        diff = jnp.where(mask, b[:, :, None] - b[:, None, :], -jnp.inf)
        A = jnp.sum(qc[:, :, None] * kc[:, None] * jnp.exp(diff), axis=-1)
        o_intra = jnp.einsum('btsh,bshv->bthv', A, vc)
        bL = b[:, -1]
        k_dec = kc * jnp.exp(bL[:, None] - b)
        S_new = jnp.exp(bL)[..., None] * S + jnp.einsum('blhk,blhv->bhkv', k_dec, vc)
        return S_new, o_inter + o_intra

    xs = tuple(_to_chunks(a.astype(f32), L) for a in (q, k, v, log_a))
    S, o = lax.scan(step, S0.astype(f32), xs)
    return _from_chunks(o), S


def _mlstm_scan(q, k, v, log_i, log_f, C0, n0, m0):
    f32 = jnp.float32
    T = q.shape[1]
    L = _chunk_len(T)
    mask = jnp.tril(jnp.ones((L, L), dtype=bool))[None, :, :, None]

    def step(carry, inp):
        C, n, m = carry
        qc, kc, vc, lic, lfc = inp
        g = jnp.cumsum(lfc, axis=1)
        D = jnp.where(mask, g[:, :, None] - g[:, None, :] + lic[:, None, :], -jnp.inf)
        w0 = g + m[:, None]
        m_t = jnp.maximum(w0, jnp.max(D, axis=2))
        qk = jnp.einsum('bthk,bshk->btsh', qc, kc) * jnp.exp(D - m_t[:, :, None])
        s0 = jnp.exp(w0 - m_t)
        num = s0[..., None] * jnp.einsum('bthk,bhkv->bthv', qc, C) + jnp.einsum('btsh,bshv->bthv', qk, vc)
        den = s0 * jnp.einsum('bthk,bhk->bth', qc, n) + jnp.sum(qk, axis=2)
        h = num / jnp.maximum(jnp.abs(den), jnp.exp(-m_t))[..., None]
        mL = m_t[:, -1]
        wL = jnp.exp(g[:, -1:] - g + lic - mL[:, None])
        sL = jnp.exp(g[:, -1] + m - mL)
        C_new = sL[..., None, None] * C + jnp.einsum('blh,blhk,blhv->bhkv', wL, kc, vc)
        n_new = sL[..., None] * n + jnp.einsum('blh,blhk->bhk', wL, kc)
        return (C_new, n_new, mL), h

    xs = tuple(_to_chunks(a.astype(f32), L) for a in (q, k, v, log_i, log_f))
    (C, n, m), h = lax.scan(step, (C0.astype(f32), n0.astype(f32), m0.astype(f32)), xs)
    return _from_chunks(h), C, n, m


def _layer(x, s_gla, s_hgrn, s_c, s_n, s_m, s_cc, s_cf, lb, norm1_g, w_in, b_in, w_gla_gate, b_gla_gate,
           gn_gla, gn_hgrn, conv_mlstm, b_mlstm_f, gn_mlstm, w_branch, w_out, norm2_g, w_up, conv_ffn, w_down):
    B, T, _ = x.shape
    f32 = jnp.float32
    dt = x.dtype
    h = _rmsnorm(x, norm1_g)
    z = jnp.einsum('btd,dn->btn', h, w_in) + b_in
    (qa, ka, va, ga, ra, qb, fb, ib, gb, qkc, vc, oc, ic, fc, gm) = jnp.split(z, SPLIT_IDX, axis=-1)

    log_a = jax.nn.log_sigmoid((ra @ w_gla_gate + b_gla_gate).astype(f32)) / GLA_TAU
    o_a, S_a = _gated_linear_scan((qa * DK_A ** -0.5).reshape(B, T, H_A, DK_A), ka.reshape(B, T, H_A, DK_A),
                                  va.reshape(B, T, H_A, DV_A), log_a.reshape(B, T, H_A, DK_A), s_gla)
    o_a = _head_rmsnorm(o_a, gn_gla) * jax.nn.silu(ga.astype(f32))

    fb32 = fb.astype(f32)
    log_f_b = jnp.logaddexp(jnp.log(lb), jnp.log1p(-lb) + jax.nn.log_sigmoid(fb32))
    k_b = (1.0 - lb) * jax.nn.sigmoid(-fb32)
    o_b, S_b = _gated_linear_scan(jax.nn.silu(qb.astype(f32)).reshape(B, T, H_B, DK_B), k_b.reshape(B, T, H_B, DK_B),
                                  ib.reshape(B, T, H_B, DV_B), log_f_b.reshape(B, T, H_B, DK_B), s_hgrn)
    o_b = _head_rmsnorm(o_b, gn_hgrn) * jax.nn.silu(gb.astype(f32))

    qk_conv, cc_new = _causal_dwconv(qkc, s_cc, conv_mlstm)
    q_c, k_c = jnp.split(jax.nn.silu(qk_conv), 2, axis=-1)
    h_c, C_new, n_new, m_new = _mlstm_scan(q_c.reshape(B, T, H_C, DK_C), (k_c * DK_C ** -0.5).reshape(B, T, H_C, DK_C),
                                           vc.reshape(B, T, H_C, DV_C), ic.astype(f32),
                                           jax.nn.log_sigmoid((fc + b_mlstm_f).astype(f32)), s_c, s_n, s_m)
    o_c = _head_rmsnorm(h_c, gn_mlstm) * jax.nn.sigmoid(oc.astype(f32))

    branches = jnp.stack([o_a, o_b, o_c], axis=2).astype(dt)
    proj = jnp.einsum('btnc,ncd->btnd', branches, w_branch)
    merged = jnp.sum(jax.nn.sigmoid(gm.reshape(B, T, N_BRANCH, D_MODEL)) * proj, axis=2)
    x = x + merged @ w_out

    u = _rmsnorm(x, norm2_g) @ w_up
    u_g, u_v = jnp.split(u, [D_FF], axis=-1)
    u_g, cf_new = _causal_dwconv(u_g, s_cf, conv_ffn)
    x = x + (jax.nn.gelu(u_g, approximate=False) * u_v) @ w_down
    return x, (S_a.astype(dt), S_b.astype(dt), C_new.astype(dt), n_new.astype(dt), m_new.astype(dt),
               cc_new.astype(dt), cf_new.astype(dt))


def _zero_states(x):
    B = x.shape[0]
    dt = x.dtype
    return (jnp.zeros((B, H_A, DK_A, DV_A), dt), jnp.zeros((B, H_B, DK_B, DV_B), dt),
            jnp.zeros((B, H_C, DK_C, DV_C), dt), jnp.zeros((B, H_C, DK_C), dt), jnp.zeros((B, H_C), dt),
            jnp.zeros((B, CONV_C - 1, 2 * H_C * DK_C), dt), jnp.zeros((B, CONV_F - 1, D_FF), dt))


def setup_inputs(seed: int = 0) -> dict:
    key = jax.random.key(seed)
    ks = jax.random.split(key, 32)

    def nrm(k, shape, scale):
        return jax.random.normal(k, shape, jnp.float32) * scale

    return {
        'x_prompt': nrm(ks[0], (BATCH, SEQ, D_MODEL), 1.0),
        'x_sample': nrm(ks[1], (DEC_BATCH, DEC_SEQ, D_MODEL), 1.0),
        'state_gla': nrm(ks[2], (DEPTH, DEC_BATCH, H_A, DK_A, DV_A), 0.5),
        'state_hgrn': nrm(ks[3], (DEPTH, DEC_BATCH, H_B, DK_B, DV_B), 0.5),
        'state_mlstm_c': nrm(ks[4], (DEPTH, DEC_BATCH, H_C, DK_C, DV_C), 0.5),
        'state_mlstm_n': nrm(ks[5], (DEPTH, DEC_BATCH, H_C, DK_C), 0.5),
        'state_mlstm_m': nrm(ks[6], (DEPTH, DEC_BATCH, H_C), 1.0),
        'state_mlstm_conv': nrm(ks[7], (DEPTH, DEC_BATCH, CONV_C - 1, 2 * H_C * DK_C), 1.0),
        'state_ffn_conv': nrm(ks[8], (DEPTH, DEC_BATCH, CONV_F - 1, D_FF), 1.0),
        'norm1_g': 1.0 + nrm(ks[9], (DEPTH, D_MODEL), 0.02),
        'w_in': nrm(ks[10], (DEPTH, D_MODEL, N_IN), D_MODEL ** -0.5),
        'b_in': nrm(ks[11], (DEPTH, N_IN), 0.02),
        'w_gla_gate': nrm(ks[12], (DEPTH, GLA_RANK, H_A * DK_A), GLA_RANK ** -0.5),
        'b_gla_gate': nrm(ks[13], (DEPTH, H_A * DK_A), 0.02),
        'gn_gla': 1.0 + nrm(ks[14], (DEPTH, H_A * DV_A), 0.02),
        'lb_logits': nrm(ks[15], (DEPTH, H_B * DK_B), 0.1),
        'gn_hgrn': 1.0 + nrm(ks[16], (DEPTH, H_B * DV_B), 0.02),
        'conv_mlstm': nrm(ks[17], (DEPTH, CONV_C, 2 * H_C * DK_C), CONV_C ** -0.5),
        'b_mlstm_f': 3.0 + nrm(ks[18], (DEPTH, H_C), 0.5),
        'gn_mlstm': 1.0 + nrm(ks[19], (DEPTH, H_C * DV_C), 0.02),
        'w_branch': nrm(ks[20], (DEPTH, N_BRANCH, BRANCH_WIDTH, D_MODEL), BRANCH_WIDTH ** -0.5),
        'w_out': nrm(ks[21], (DEPTH, D_MODEL, D_MODEL), D_MODEL ** -0.5),
        'norm2_g': 1.0 + nrm(ks[22], (DEPTH, D_MODEL), 0.02),
        'w_up': nrm(ks[23], (DEPTH, D_MODEL, 2 * D_FF), D_MODEL ** -0.5),
        'conv_ffn': nrm(ks[24], (DEPTH, CONV_F, D_FF), CONV_F ** -0.5),
        'w_down': nrm(ks[25], (DEPTH, D_FF, D_MODEL), D_FF ** -0.5),
        'final_g': 1.0 + nrm(ks[26], (D_MODEL,), 0.02),
    }


def reference(x_prompt, x_sample, state_gla, state_hgrn, state_mlstm_c, state_mlstm_n, state_mlstm_m,
              state_mlstm_conv, state_ffn_conv, norm1_g, w_in, b_in, w_gla_gate, b_gla_gate, gn_gla, lb_logits,
              gn_hgrn, conv_mlstm, b_mlstm_f, gn_mlstm, w_branch, w_out, norm2_g, w_up, conv_ffn, w_down, final_g):
    lb_cum = jnp.cumsum(jax.nn.softmax(lb_logits.astype(jnp.float32), axis=0), axis=0)
    lower = lb_cum - lb_cum[0]

    def run(x, states):
        new = []
        for l in range(DEPTH):
            st = _zero_states(x) if states is None else tuple(s[l] for s in states)
            x, ns = _layer(x, *st, lower[l], norm1_g[l], w_in[l], b_in[l], w_gla_gate[l], b_gla_gate[l],
                           gn_gla[l], gn_hgrn[l], conv_mlstm[l], b_mlstm_f[l], gn_mlstm[l], w_branch[l], w_out[l],
                           norm2_g[l], w_up[l], conv_ffn[l], w_down[l])
            new.append(ns)
        stacked = [jnp.stack([ns[i] for ns in new], axis=0) for i in range(7)]
        return _rmsnorm(x, final_g), stacked

    y_prompt, (p_gla, p_hgrn, p_c, p_n, p_m, p_cc, p_cf) = run(x_prompt, None)
    y_sample, (s_gla, s_hgrn, s_c, s_n, s_m, s_cc, s_cf) = run(
        x_sample, (state_gla, state_hgrn, state_mlstm_c, state_mlstm_n, state_mlstm_m, state_mlstm_conv, state_ffn_conv))
    return (y_prompt, y_sample, p_gla, p_hgrn, p_c, p_n, p_m, p_cc, p_cf, s_gla, s_hgrn, s_c, s_n, s_m, s_cc, s_cf)
```

```python
import functools
import math

import jax
import jax.numpy as jnp
from jax import lax
from jax.experimental import pallas as pl
from jax.experimental.pallas import tpu as pltpu

F32 = jnp.float32
BF16 = jnp.bfloat16

D_MODEL = 1024
DEPTH = 2
CHUNK = 64
BRANCH_WIDTH = D_MODEL // 2
H_A, DV_A = 4, BRANCH_WIDTH // 4
DK_A = DV_A // 2
GLA_RANK = 16
GLA_TAU = 16.0
H_B = 4
DK_B = DV_B = BRANCH_WIDTH // H_B
H_C = 4
DK_C = DV_C = BRANCH_WIDTH // H_C
CONV_C = 4
N_BRANCH = 3
D_FF = ((8 * D_MODEL // 3 + 127) // 128) * 128
CONV_F = 3
EPS = 1e-6

SPLIT_SIZES = (
    H_A * DK_A, H_A * DK_A, H_A * DV_A, H_A * DV_A, GLA_RANK,
    H_B * DK_B, H_B * DK_B, H_B * DV_B, H_B * DV_B,
    2 * H_C * DK_C, H_C * DV_C, H_C * DV_C, H_C, H_C,
    N_BRANCH * D_MODEL,
)
SPLIT_IDX = tuple(int(sum(SPLIT_SIZES[:i + 1])) for i in range(len(SPLIT_SIZES) - 1))

LANE = 128
SUBLANE = 8
VMEM_LIMIT_BYTES = 60 * 1024 * 1024

HP = LANE
N_LIN = H_A + H_B
W_LIN = N_LIN * HP
O_Q = 0
O_K = O_Q + W_LIN
O_V = O_K + W_LIN
O_G = O_V + W_LIN
O_RA = O_G + W_LIN
O_QKC = O_RA + LANE
O_VC = O_QKC + 2 * H_C * DK_C
O_OC = O_VC + H_C * DV_C
O_IF = O_OC + H_C * DV_C
O_GM = O_IF + LANE
N_PACK = O_GM + N_BRANCH * D_MODEL
W_QKC = 2 * H_C * DK_C
W_C = H_C * DV_C
CONV_C_ROW0 = SUBLANE - (CONV_C - 1)
CONV_F_ROW0 = SUBLANE - (CONV_F - 1)


def _sigmoid(x):
    return 1.0 / (1.0 + jnp.exp(-x))


def _silu(x):
    return x * _sigmoid(x)


def _log_sigmoid(x):
    return jnp.minimum(x, 0.0) - jnp.log1p(jnp.exp(-jnp.abs(x)))


def _rmsnorm_rows(x, g):
    return x * lax.rsqrt(jnp.mean(x * x, axis=-1, keepdims=True) + EPS) * g


def _dot(a, b):
    return jnp.dot(a, b, preferred_element_type=F32)


def _dot_nt(a, b):
    return lax.dot_general(a, b, (((1,), (1,)), ((), ())), preferred_element_type=F32)


def _dot_tn(a, b):
    return lax.dot_general(a, b, (((0,), (0,)), ((), ())), preferred_element_type=F32)


def _cumsum_rows(tri, a):
    hi = a.astype(BF16)
    r1 = a - hi.astype(F32)
    mid = r1.astype(BF16)
    lo = (r1 - mid.astype(F32)).astype(BF16)
    return _dot(tri, hi) + _dot(tri, mid) + _dot(tri, lo)


def _mixer_kernel(*refs, L, n_chunks, n_tblocks, stateful):
    it = iter(refs)
    x_ref = next(it)
    if stateful:
        sgla_in, shgrn_in, sc_in, sn_in, sm_in, scc_in = (next(it) for _ in range(6))
    (g1_ref, win_ref, bin_ref, wg_ref, bg_ref, gnlin_ref, loglb_ref, log1mlb_ref, onemlb_ref,
     convm_ref, bf_ref, gnm_ref, wbr_ref, wout_ref) = (next(it) for _ in range(14))
    x1_ref, sgla_out, shgrn_out, sc_out, sn_out, sm_out, scc_out = (next(it) for _ in range(7))
    (h_s, q_s, k_s, la_s, v_s, g_s, qkp_s, vc_s, oc_s, if_s, o_s,
     st_s, c_s, n_s, m_s, conv_s, cbuf_s) = (next(it) for _ in range(17))

    t_idx = pl.program_id(1)

    x = x_ref[...]
    hb = _rmsnorm_rows(x, g1_ref[...]).astype(BF16)
    h_s[...] = hb

    def seg(off, width):
        return _dot(hb, win_ref[:, off:off + width]) + bin_ref[:, off:off + width]

    wa = H_A * HP
    zq = seg(O_Q, W_LIN)
    q_s[:, :wa] = zq[:, :wa] * (DK_A ** -0.5)
    q_s[:, wa:] = _silu(zq[:, wa:])
    zk = seg(O_K, W_LIN)
    k_s[:, :wa] = zk[:, :wa]
    fb = zk[:, wa:]
    u = jnp.exp(-jnp.abs(fb))
    log_sig = jnp.minimum(fb, 0.0) - jnp.log1p(u)
    a_lb = loglb_ref[...]
    c_lb = log1mlb_ref[...] + log_sig
    la_s[:, wa:] = jnp.maximum(a_lb, c_lb) + jnp.log1p(jnp.exp(-jnp.abs(a_lb - c_lb)))
    k_s[:, wa:] = onemlb_ref[...] * (jnp.where(fb >= 0.0, u, 1.0) / (1.0 + u))
    ra = seg(O_RA, LANE).astype(BF16)
    la_s[:, :wa] = _log_sigmoid(_dot(ra, wg_ref[...]) + bg_ref[...]) * (1.0 / GLA_TAU)
    v_s[...] = seg(O_V, W_LIN)
    g_s[...] = _silu(seg(O_G, W_LIN))
    qkp_s[...] = seg(O_QKC, W_QKC)
    vc_s[...] = seg(O_VC, W_C)
    oc_s[...] = _sigmoid(seg(O_OC, W_C))
    zif = seg(O_IF, LANE)
    lane = lax.broadcasted_iota(jnp.int32, zif.shape, 1)
    if_s[...] = jnp.where(lane >= H_C, _log_sigmoid(zif + bf_ref[...]), zif)

    def zero_state():
        st_s[...] = jnp.zeros_like(st_s)
        c_s[...] = jnp.zeros_like(c_s)
        n_s[...] = jnp.zeros_like(n_s)
        m_s[...] = jnp.zeros_like(m_s)
        conv_s[...] = jnp.zeros_like(conv_s)

    def load_state(slot):
        for h in range(H_A):
            s = sgla_in[slot, h]
            s = jnp.concatenate([s, jnp.zeros((HP - DK_A, DV_A), F32)], axis=0)
            st_s[h] = s.T
        for h in range(H_B):
            st_s[H_A + h] = shgrn_in[slot, h].T
        c_s[...] = sc_in[slot]
        n_s[0:H_C, :] = sn_in[slot]
        m_s[...] = sm_in[slot]
        conv_s[...] = scc_in[slot]

    def store_state(slot):
        for h in range(H_A):
            sgla_out[slot, h] = st_s[h].T[:DK_A, :]
        for h in range(H_B):
            shgrn_out[slot, h] = st_s[H_A + h].T
        sc_out[slot] = c_s[...]
        sn_out[slot] = n_s[0:H_C, :]
        sm_out[slot] = m_s[...]
        scc_out[slot] = conv_s[...]

    if not stateful:
        pl.when(t_idx == 0)(zero_state)

    row_i = lax.broadcasted_iota(jnp.int32, (L, L), 0)
    col_i = lax.broadcasted_iota(jnp.int32, (L, L), 1)
    tril = row_i >= col_i
    tri = jnp.where(tril, 1.0, 0.0).astype(BF16)

    def chunk_body(c, carry):
        if stateful:
            load_state(c)
        rows = pl.ds(pl.multiple_of(c * L, L), L)

        b = _cumsum_rows(tri, la_s[rows, :])
        b_last = b[L - 1:L, :]
        qt = q_s[rows, :] * jnp.exp(b)
        kk = k_s[rows, :]
        kt = kk * jnp.exp(-b)
        kdec = kk * jnp.exp(b_last - b)
        dec_last = jnp.exp(b_last)
        vv = v_s[rows, :]
        gate = g_s[rows, :]
        gn = gnlin_ref[...]
        for h in range(N_LIN):
            cols = slice(h * HP, (h + 1) * HP)
            vcols = slice(h * DV_A, (h + 1) * DV_A)
            qh = qt[:, cols].astype(BF16)
            kh = kt[:, cols].astype(BF16)
            vh = vv[:, vcols].astype(BF16)
            att = jnp.where(tril, _dot_nt(qh, kh), 0.0)
            st = st_s[h]
            o = _dot_nt(qh, st.astype(BF16)) + _dot(att.astype(BF16), vh)
            st_s[h] = st * dec_last[:, cols] + _dot_tn(vh, kdec[:, cols].astype(BF16))
            o = o * lax.rsqrt(jnp.mean(o * o, axis=-1, keepdims=True) + EPS)
            o_s[rows, vcols] = (o * gn[:, vcols] * gate[:, vcols]).astype(BF16)

        cbuf_s[0:SUBLANE, :] = conv_s[...]
        cbuf_s[SUBLANE:SUBLANE + L, :] = qkp_s[rows, :]
        conv_s[...] = cbuf_s[L:L + SUBLANE, :]
        wconv = convm_ref[...]
        acc = cbuf_s[CONV_C_ROW0:CONV_C_ROW0 + L, :] * wconv[0:1, :]
        for j in range(1, CONV_C):
            acc = acc + cbuf_s[CONV_C_ROW0 + j:CONV_C_ROW0 + j + L, :] * wconv[j:j + 1, :]
        qk = _silu(acc)
        gates = if_s[rows, :]
        gcum = _cumsum_rows(tri, gates)
        pad = LANE - L
        if pad:
            gates_t = jnp.concatenate([gates, jnp.zeros((pad, LANE), F32)], axis=0).T[:, :L]
            gcum_t = jnp.concatenate([gcum, jnp.zeros((pad, LANE), F32)], axis=0).T[:, :L]
        else:
            gates_t = gates.T
            gcum_t = gcum.T
        vcv = vc_s[rows, :]
        ocv = oc_s[rows, :]
        gnm = gnm_ref[...]
        for h in range(H_C):
            cols = slice(h * DK_C, (h + 1) * DK_C)
            qh = qk[:, cols]
            kh = qk[:, H_C * DK_C + h * DK_C:H_C * DK_C + (h + 1) * DK_C] * (DK_C ** -0.5)
            vh = vcv[:, cols].astype(BF16)
            g_col = gcum[:, H_C + h:H_C + h + 1]
            li_col = gates[:, h:h + 1]
            g_row = gcum_t[H_C + h:H_C + h + 1, :]
            li_row = gates_t[h:h + 1, :]
            m_prev = m_s[h:h + 1, 0:1]
            dmat = jnp.where(tril, g_col - g_row + li_row, -jnp.inf)
            w0 = g_col + m_prev
            m_t = jnp.maximum(w0, jnp.max(dmat, axis=1, keepdims=True))
            qkm = _dot_nt(qh.astype(BF16), kh.astype(BF16)) * jnp.exp(dmat - m_t)
            s0 = jnp.exp(w0 - m_t)
            cst = c_s[h]
            nrow = n_s[h:h + 1, :]
            num = s0 * _dot(qh.astype(BF16), cst.astype(BF16)) + _dot(qkm.astype(BF16), vh)
            den = (s0 * jnp.sum(qh * nrow, axis=-1, keepdims=True)
                   + jnp.sum(qkm, axis=-1, keepdims=True))
            hh = num / jnp.maximum(jnp.abs(den), jnp.exp(-m_t))
            m_last = m_t[L - 1:L, :]
            g_last = g_col[L - 1:L, :]
            w_l = jnp.exp(g_last - g_col + li_col - m_last)
            s_l = jnp.exp(g_last + m_prev - m_last)
            kw = kh * w_l
            c_s[h] = s_l * cst + _dot_tn(kw.astype(BF16), vh)
            n_s[h:h + 1, :] = s_l * nrow + jnp.sum(kw, axis=0, keepdims=True)
            m_s[h:h + 1, :] = jnp.broadcast_to(m_last, (1, LANE))
            hh = hh * lax.rsqrt(jnp.mean(hh * hh, axis=-1, keepdims=True) + EPS)
            o_s[rows, W_LIN + h * DV_C:W_LIN + (h + 1) * DV_C] = (
                hh * gnm[:, cols] * ocv[:, cols]).astype(BF16)

        if stateful:
            store_state(c)
        return carry

    lax.fori_loop(0, n_chunks, chunk_body, 0)

    if not stateful:
        pl.when(t_idx == n_tblocks - 1)(functools.partial(store_state, 0))

    hb = h_s[...]
    merged = None
    for n in range(N_BRANCH):
        proj = _dot(o_s[:, n * BRANCH_WIDTH:(n + 1) * BRANCH_WIDTH], wbr_ref[n])
        gm = seg(O_GM + n * D_MODEL, D_MODEL)
        term = _sigmoid(gm) * proj
        merged = term if merged is None else merged + term
    x1_ref[...] = x_ref[...] + _dot(merged.astype(BF16), wout_ref[...])


def _const_spec(shape):
    nd = len(shape)
    return pl.BlockSpec(shape, lambda i, j, _nd=nd: (0,) * _nd)


def _mixer_call(x2d, states, params, *, n_streams, t_len, streams_per_step, tt):
    stateful = states is not None
    if stateful:
        L = t_len
        assert L % SUBLANE == 0 and L <= CHUNK
        n_chunks = streams_per_step
        tt = streams_per_step * L
        n_tblocks = 1
        grid = (n_streams // streams_per_step, 1)
        ns = streams_per_step
    else:
        L = CHUNK
        assert t_len % tt == 0 and tt % L == 0
        n_chunks = tt // L
        n_tblocks = t_len // tt
        grid = (n_streams, n_tblocks)
        ns = 1

    x_spec = pl.BlockSpec((tt, D_MODEL), lambda i, j: (i * n_tblocks + j, 0))
    state_shapes = [
        (n_streams, H_A, DK_A, DV_A), (n_streams, H_B, DK_B, DV_B), (n_streams, H_C, DK_C, DV_C),
        (n_streams, H_C, DK_C), (n_streams, SUBLANE, LANE), (n_streams, SUBLANE, W_QKC),
    ]

    def state_spec(shape):
        nd = len(shape)
        return pl.BlockSpec((ns,) + shape[1:], lambda i, j, _nd=nd: (i,) + (0,) * (_nd - 1))

    in_specs = [x_spec]
    args = [x2d]
    if stateful:
        in_specs += [state_spec(s) for s in state_shapes]
        args += list(states)
    in_specs += [_const_spec(p.shape) for p in params]
    args += list(params)

    out_shape = [jax.ShapeDtypeStruct(x2d.shape, F32)] + [jax.ShapeDtypeStruct(s, F32) for s in state_shapes]
    out_specs = [x_spec] + [state_spec(s) for s in state_shapes]

    scratch = [
        pltpu.VMEM((tt, D_MODEL), BF16),
        pltpu.VMEM((tt, W_LIN), F32),
        pltpu.VMEM((tt, W_LIN), F32),
        pltpu.VMEM((tt, W_LIN), F32),
        pltpu.VMEM((tt, W_LIN), F32),
        pltpu.VMEM((tt, W_LIN), F32),
        pltpu.VMEM((tt, W_QKC), F32),
        pltpu.VMEM((tt, W_C), F32),
        pltpu.VMEM((tt, W_C), F32),
        pltpu.VMEM((tt, LANE), F32),
        pltpu.VMEM((tt, N_BRANCH * BRANCH_WIDTH), BF16),
        pltpu.VMEM((N_LIN, HP, HP), F32),
        pltpu.VMEM((H_C, DK_C, DV_C), F32),
        pltpu.VMEM((SUBLANE, LANE), F32),
        pltpu.VMEM((SUBLANE, LANE), F32),
        pltpu.VMEM((SUBLANE, W_QKC), F32),
        pltpu.VMEM((SUBLANE + L, W_QKC), F32),
    ]
    kern = functools.partial(_mixer_kernel, L=L, n_chunks=n_chunks, n_tblocks=n_tblocks, stateful=stateful)
    return pl.pallas_call(
        kern,
        grid=grid,
        in_specs=in_specs,
        out_specs=out_specs,
        out_shape=out_shape,
        scratch_shapes=scratch,
        compiler_params=pltpu.CompilerParams(
            dimension_semantics=("arbitrary", "arbitrary"), vmem_limit_bytes=VMEM_LIMIT_BYTES),
        name="mixer_sample" if stateful else "mixer_prompt",
    )(*args)


def _ffn_kernel(*refs, n_sub, ls, n_tblocks, stateful, final):
    it = iter(refs)
    x_ref = next(it)
    if stateful:
        scf_in = next(it)
    g2_ref, wup_ref, convf_ref, wdown_ref, gfin_ref = (next(it) for _ in range(5))
    y_ref, scf_out = next(it), next(it)
    hist_s, cbuf_s, act_s = next(it), next(it), next(it)

    t_idx = pl.program_id(1)
    x = x_ref[...]
    hb = _rmsnorm_rows(x, g2_ref[...]).astype(BF16)
    ug = _dot(hb, wup_ref[:, :D_FF])
    uv = _dot(hb, wup_ref[:, D_FF:])

    if not stateful:
        @pl.when(t_idx == 0)
        def _():
            hist_s[...] = jnp.zeros_like(hist_s)

    wconv = convf_ref[...]
    for s in range(n_sub):
        r0 = s * ls
        cbuf_s[0:SUBLANE, :] = scf_in[s] if stateful else hist_s[...]
        cbuf_s[SUBLANE:SUBLANE + ls, :] = ug[r0:r0 + ls, :]
        new_hist = cbuf_s[ls:ls + SUBLANE, :]
        if stateful:
            scf_out[s] = new_hist
        else:
            hist_s[...] = new_hist
        acc = cbuf_s[CONV_F_ROW0:CONV_F_ROW0 + ls, :] * wconv[0:1, :]
        for j in range(1, CONV_F):
            acc = acc + cbuf_s[CONV_F_ROW0 + j:CONV_F_ROW0 + j + ls, :] * wconv[j:j + 1, :]
        gel = 0.5 * acc * (1.0 + lax.erf(acc * (2.0 ** -0.5)))
        act_s[r0:r0 + ls, :] = (gel * uv[r0:r0 + ls, :]).astype(BF16)

    if not stateful:
        @pl.when(t_idx == n_tblocks - 1)
        def _():
            scf_out[0] = hist_s[...]

    x2 = x + _dot(act_s[...], wdown_ref[...])
    if final:
        x2 = _rmsnorm_rows(x2, gfin_ref[...])
    y_ref[...] = x2


def _ffn_call(x2d, state, params, *, n_streams, t_len, streams_per_step, tt, final):
    stateful = state is not None
    if stateful:
        ls = t_len
        n_sub = streams_per_step
        tt = n_sub * ls
        n_tblocks = 1
        grid = (n_streams // n_sub, 1)
    else:
        assert t_len % tt == 0
        ls = tt
        n_sub = 1
        n_tblocks = t_len // tt
        grid = (n_streams, n_tblocks)

    x_spec = pl.BlockSpec((tt, D_MODEL), lambda i, j: (i * n_tblocks + j, 0))
    st_shape = (n_streams, SUBLANE, D_FF)
    st_spec = pl.BlockSpec((n_sub, SUBLANE, D_FF), lambda i, j: (i, 0, 0))
    in_specs = [x_spec]
    args = [x2d]
    if stateful:
        in_specs.append(st_spec)
        args.append(state)
    in_specs += [_const_spec(p.shape) for p in params]
    args += list(params)
    scratch = [
        pltpu.VMEM((SUBLANE, D_FF), F32),
        pltpu.VMEM((SUBLANE + ls, D_FF), F32),
        pltpu.VMEM((tt, D_FF), BF16),
    ]
    kern = functools.partial(_ffn_kernel, n_sub=n_sub, ls=ls, n_tblocks=n_tblocks, stateful=stateful, final=final)
    return pl.pallas_call(
        kern,
        grid=grid,
        in_specs=in_specs,
        out_specs=[x_spec, st_spec],
        out_shape=[jax.ShapeDtypeStruct(x2d.shape, F32), jax.ShapeDtypeStruct(st_shape, F32)],
        scratch_shapes=scratch,
        compiler_params=pltpu.CompilerParams(
            dimension_semantics=("arbitrary", "arbitrary"), vmem_limit_bytes=VMEM_LIMIT_BYTES),
        name="ffn_sample" if stateful else "ffn_prompt",
    )(*args)


def _pad_heads(a):
    r = a.shape[0]
    a = a.reshape(r, H_A, DK_A)
    return jnp.pad(a, ((0, 0), (0, 0), (0, HP - DK_A))).reshape(r, H_A * HP)


def _pack_cols(w):
    qa, ka, va, ga, ra, qb, fb, ib, gb, qkc, vc, oc, ic, fc, gm = jnp.split(w, SPLIT_IDX, axis=-1)
    ra_p = jnp.pad(ra, ((0, 0), (0, LANE - GLA_RANK)))
    if_p = jnp.pad(jnp.concatenate([ic, fc], axis=-1), ((0, 0), (0, LANE - 2 * H_C)))
    return jnp.concatenate(
        [_pad_heads(qa), qb, _pad_heads(ka), fb, va, ib, ga, gb, ra_p, qkc, vc, oc, if_p, gm], axis=-1)


def _layer_params(l, lower, norm1_g, w_in, b_in, w_gla_gate, b_gla_gate, gn_gla, gn_hgrn, conv_mlstm,
                  b_mlstm_f, gn_mlstm, w_branch, w_out, norm2_g, w_up, conv_ffn, w_down, final_g):
    lb = lower[l][None, :]
    wg = jnp.pad(_pad_heads(w_gla_gate[l]), ((0, LANE - GLA_RANK), (0, 0))).astype(BF16)
    bfrow = jnp.pad(b_mlstm_f[l][None, :], ((0, 0), (H_C, LANE - 2 * H_C)))
    mixer = (
        norm1_g[l][None, :], _pack_cols(w_in[l]).astype(BF16), _pack_cols(b_in[l][None, :]),
        wg, _pad_heads(b_gla_gate[l][None, :]),
        jnp.concatenate([gn_gla[l], gn_hgrn[l]])[None, :],
        jnp.log(lb), jnp.log1p(-lb), 1.0 - lb,
        conv_mlstm[l], bfrow, gn_mlstm[l][None, :],
        w_branch[l].astype(BF16), w_out[l].astype(BF16),
    )
    ffn = (norm2_g[l][None, :], w_up[l].astype(BF16), conv_ffn[l], w_down[l].astype(BF16), final_g[None, :])
    return mixer, ffn


def _run_group(x, states, layer_params, *, streams_per_step, tt_mixer, tt_ffn):
    n_streams, t_len, _ = x.shape
    x2d = x.reshape(n_streams * t_len, D_MODEL)
    new_states = []
    for l in range(DEPTH):
        mixer_p, ffn_p = layer_params[l]
        if states is None:
            mix_in, ffn_in = None, None
        else:
            s_gla, s_hgrn, s_c, s_n, s_m, s_cc, s_cf = (s[l] for s in states)
            m_tile = jnp.broadcast_to(jnp.pad(s_m, ((0, 0), (0, SUBLANE - H_C)))[:, :, None],
                                      (n_streams, SUBLANE, LANE))
            cc_tile = jnp.pad(s_cc, ((0, 0), (CONV_C_ROW0, 0), (0, 0)))
            mix_in = (s_gla, s_hgrn, s_c, s_n, m_tile, cc_tile)
            ffn_in = jnp.pad(s_cf, ((0, 0), (CONV_F_ROW0, 0), (0, 0)))
        x2d, o_gla, o_hgrn, o_c, o_n, o_m, o_cc = _mixer_call(
            x2d, mix_in, mixer_p, n_streams=n_streams, t_len=t_len,
            streams_per_step=streams_per_step, tt=tt_mixer)
        x2d, o_cf = _ffn_call(
            x2d, ffn_in, ffn_p, n_streams=n_streams, t_len=t_len,
            streams_per_step=streams_per_step, tt=tt_ffn, final=(l == DEPTH - 1))
        new_states.append((o_gla, o_hgrn, o_c, o_n, o_m[:, :H_C, 0], o_cc[:, CONV_C_ROW0:, :],
                           o_cf[:, CONV_F_ROW0:, :]))
    stacked = [jnp.stack([ns[i] for ns in new_states], axis=0) for i in range(7)]
    return x2d.reshape(n_streams, t_len, D_MODEL), stacked


def kernel(x_prompt, x_sample, state_gla, state_hgrn, state_mlstm_c, state_mlstm_n, state_mlstm_m,
           state_mlstm_conv, state_ffn_conv, norm1_g, w_in, b_in, w_gla_gate, b_gla_gate, gn_gla, lb_logits,
           gn_hgrn, conv_mlstm, b_mlstm_f, gn_mlstm, w_branch, w_out, norm2_g, w_up, conv_ffn, w_down, final_g):
    lb_cum = jnp.cumsum(jax.nn.softmax(lb_logits.astype(F32), axis=0), axis=0)
    lower = lb_cum - lb_cum[0]
    layer_params = [
        _layer_params(l, lower, norm1_g, w_in, b_in, w_gla_gate, b_gla_gate, gn_gla, gn_hgrn, conv_mlstm,
                      b_mlstm_f, gn_mlstm, w_branch, w_out, norm2_g, w_up, conv_ffn, w_down, final_g)
        for l in range(DEPTH)
    ]
    y_p, p_states = _run_group(x_prompt, None, layer_params, streams_per_step=1, tt_mixer=256, tt_ffn=256)
    sample_states = (state_gla, state_hgrn, state_mlstm_c, state_mlstm_n, state_mlstm_m,
                     state_mlstm_conv, state_ffn_conv)
    y_s, s_states = _run_group(x_sample, sample_states, layer_params, streams_per_step=4,
                               tt_mixer=None, tt_ffn=None)
    return (y_p, y_s, *p_states, *s_states)
```

```python
import functools

import jax
import jax.numpy as jnp
from jax import lax
from jax.experimental import pallas as pl
from jax.experimental.pallas import tpu as pltpu

F32 = jnp.float32
BF16 = jnp.bfloat16

D_MODEL = 1024
DEPTH = 2
CHUNK = 64
BRANCH_WIDTH = D_MODEL // 2
H_A, DV_A = 4, BRANCH_WIDTH // 4
DK_A = DV_A // 2
GLA_RANK = 16
GLA_TAU = 16.0
H_B = 4
DK_B = DV_B = BRANCH_WIDTH // H_B
H_C = 4
DK_C = DV_C = BRANCH_WIDTH // H_C
CONV_C = 4
N_BRANCH = 3
D_FF = ((8 * D_MODEL // 3 + 127) // 128) * 128
CONV_F = 3
EPS = 1e-6

SPLIT_SIZES = (
    H_A * DK_A, H_A * DK_A, H_A * DV_A, H_A * DV_A, GLA_RANK,
    H_B * DK_B, H_B * DK_B, H_B * DV_B, H_B * DV_B,
    2 * H_C * DK_C, H_C * DV_C, H_C * DV_C, H_C, H_C,
    N_BRANCH * D_MODEL,
)
SPLIT_IDX = tuple(int(sum(SPLIT_SIZES[:i + 1])) for i in range(len(SPLIT_SIZES) - 1))

LANE = 128
SUBLANE = 8
VMEM_LIMIT_BYTES = 60 * 1024 * 1024

HP = LANE
N_LIN = H_A + H_B
W_LIN = N_LIN * HP
O_Q = 0
O_K = O_Q + W_LIN
O_V = O_K + W_LIN
O_G = O_V + W_LIN
O_RA = O_G + W_LIN
O_QKC = O_RA + LANE
O_VC = O_QKC + 2 * H_C * DK_C
O_OC = O_VC + H_C * DV_C
O_IF = O_OC + H_C * DV_C
O_GM = O_IF + LANE
N_PACK = O_GM + N_BRANCH * D_MODEL
W_QKC = 2 * H_C * DK_C
W_C = H_C * DV_C
CONV_C_ROW0 = SUBLANE - (CONV_C - 1)
CONV_F_ROW0 = SUBLANE - (CONV_F - 1)


def _sigmoid(x):
    return 1.0 / (1.0 + jnp.exp(-x))


def _silu(x):
    return x * _sigmoid(x)


def _log_sigmoid(x):
    return jnp.minimum(x, 0.0) - jnp.log1p(jnp.exp(-jnp.abs(x)))


def _rmsnorm_rows(x, g):
    return x * lax.rsqrt(jnp.mean(x * x, axis=-1, keepdims=True) + EPS) * g


def _dot(a, b):
    return jnp.dot(a, b, preferred_element_type=F32)


def _dot_nt(a, b):
    return lax.dot_general(a, b, (((1,), (1,)), ((), ())), preferred_element_type=F32)


def _dot_tn(a, b):
    return lax.dot_general(a, b, (((0,), (0,)), ((), ())), preferred_element_type=F32)


def _cumsum_rows(tri, a):
    hi = a.astype(BF16)
    r1 = a - hi.astype(F32)
    mid = r1.astype(BF16)
    lo = (r1 - mid.astype(F32)).astype(BF16)
    return _dot(tri, hi) + _dot(tri, mid) + _dot(tri, lo)


def _mixer_kernel(*refs, L, n_chunks, n_tblocks, stateful):
    it = iter(refs)
    x_ref = next(it)
    if stateful:
        sgla_in, shgrn_in, sc_in, sn_in, sm_in, scc_in = (next(it) for _ in range(6))
    (g1_ref, win_ref, bin_ref, wg_ref, bg_ref, gnlin_ref, loglb_ref, log1mlb_ref, onemlb_ref,
     convm_ref, bf_ref, gnm_ref, wbr_ref, wout_ref) = (next(it) for _ in range(14))
    x1_ref, sgla_out, shgrn_out, sc_out, sn_out, sm_out, scc_out = (next(it) for _ in range(7))
    (h_s, k_s, la_s, g_s, qkp_s, vc_s, oc_s, o_s, qt_s, kt_s, kdec_s, vb_s,
     st_s, c_s, n_s, m_s, conv_s, cbuf_s) = (next(it) for _ in range(18))

    t_idx = pl.program_id(1)
    tt = n_chunks * L
    last_c = n_chunks - 1
    wa = H_A * HP

    if not stateful:
        @pl.when(t_idx == 0)
        def _():
            st_s[...] = jnp.zeros_like(st_s)
            c_s[...] = jnp.zeros_like(c_s)
            n_s[...] = jnp.zeros_like(n_s)
            m_s[...] = jnp.zeros_like(m_s)
            conv_s[...] = jnp.zeros_like(conv_s)

    row_i = lax.broadcasted_iota(jnp.int32, (L, L), 0)
    col_i = lax.broadcasted_iota(jnp.int32, (L, L), 1)
    tril = row_i >= col_i
    r_i = lax.broadcasted_iota(jnp.int32, (tt, tt), 0)
    c_i = lax.broadcasted_iota(jnp.int32, (tt, tt), 1)
    chunk_start = (r_i // L) * L
    tri_bd = jnp.where(r_i >= c_i, jnp.where(c_i >= chunk_start, 1.0, 0.0), 0.0).astype(BF16)

    x = x_ref[...]
    hb = _rmsnorm_rows(x, g1_ref[...]).astype(BF16)
    h_s[...] = hb

    def seg(off, width):
        return _dot(hb, win_ref[:, off:off + width]) + bin_ref[:, off:off + width]

    ra = seg(O_RA, LANE).astype(BF16)
    la_s[:, :wa] = _log_sigmoid(_dot(ra, wg_ref[...]) + bg_ref[...]) * (1.0 / GLA_TAU)
    zk = seg(O_K, W_LIN)
    k_s[:, :wa] = zk[:, :wa]
    fb = zk[:, wa:]
    u = jnp.exp(-jnp.abs(fb))
    log_sig = jnp.minimum(fb, 0.0) - jnp.log1p(u)
    a_lb = loglb_ref[...]
    c_lb = log1mlb_ref[...] + log_sig
    la_s[:, wa:] = jnp.maximum(a_lb, c_lb) + jnp.log1p(jnp.exp(-jnp.abs(a_lb - c_lb)))
    k_s[:, wa:] = onemlb_ref[...] * (jnp.where(fb >= 0.0, u, 1.0) / (1.0 + u))

    b = _cumsum_rows(tri_bd, la_s[...])
    b3 = b.reshape(n_chunks, L, W_LIN)
    b_last = b3[:, L - 1:L, :]
    dec_last = jnp.exp(b_last)
    zq = seg(O_Q, W_LIN)
    eb = jnp.exp(b)
    qt_s[:, :wa] = (zq[:, :wa] * (DK_A ** -0.5) * eb[:, :wa]).astype(BF16)
    qt_s[:, wa:] = (_silu(zq[:, wa:]) * eb[:, wa:]).astype(BF16)
    kk = k_s[...]
    kt_s[...] = (kk * jnp.exp(-b)).astype(BF16)
    kdec_s[...] = (kk.reshape(n_chunks, L, W_LIN) * jnp.exp(b_last - b3)).reshape(tt, W_LIN).astype(BF16)
    vb_s[...] = seg(O_V, W_LIN).astype(BF16)
    g_s[...] = _silu(seg(O_G, W_LIN))
    qkp_s[...] = seg(O_QKC, W_QKC)
    vc_s[...] = seg(O_VC, W_C).astype(BF16)
    oc_s[...] = _sigmoid(seg(O_OC, W_C))
    zif = seg(O_IF, LANE)
    lane = lax.broadcasted_iota(jnp.int32, zif.shape, 1)
    gates_all = jnp.where(lane >= H_C, _log_sigmoid(zif + bf_ref[...]), zif)
    gcum_all = _cumsum_rows(tri_bd, gates_all)

    n_sub = n_chunks if stateful else 1
    ls = tt // n_sub
    wconv = convm_ref[...]
    for s in range(n_sub):
        r0 = s * ls
        cbuf_s[0:SUBLANE, :] = scc_in[s] if stateful else conv_s[...]
        cbuf_s[SUBLANE:SUBLANE + ls, :] = qkp_s[r0:r0 + ls, :]
        new_hist = cbuf_s[ls:ls + SUBLANE, :]
        if stateful:
            scc_out[s] = new_hist
        else:
            conv_s[...] = new_hist
        acc = cbuf_s[CONV_C_ROW0:CONV_C_ROW0 + ls, :] * wconv[0:1, :]
        for j in range(1, CONV_C):
            acc = acc + cbuf_s[CONV_C_ROW0 + j:CONV_C_ROW0 + j + ls, :] * wconv[j:j + 1, :]
        qk = _silu(acc)
        qkp_s[r0:r0 + ls, :W_C] = qk[:, :W_C]
        qkp_s[r0:r0 + ls, W_C:] = qk[:, W_C:] * (DK_C ** -0.5)

    def lin_state_in(c, h):
        if not stateful:
            return st_s[h]
        if h < H_A:
            return jnp.concatenate([sgla_in[c, h], jnp.zeros((HP - DK_A, DV_A), F32)], axis=0).T
        return shgrn_in[c, h - H_A].T

    def lin_state_out(c, h, s):
        if not stateful:
            st_s[h] = s
        elif h < H_A:
            sgla_out[c, h] = s.T[:DK_A, :]
        else:
            shgrn_out[c, h - H_A] = s.T

    gn = gnlin_ref[...]
    for h in range(N_LIN):
        cols = slice(h * HP, (h + 1) * HP)
        att, upd, o_inter = [], [], []
        for c in range(n_chunks):
            rows = slice(c * L, (c + 1) * L)
            a = _dot_nt(qt_s[rows, cols], kt_s[rows, cols])
            att.append(jnp.where(tril, a, 0.0).astype(BF16))
        for c in range(n_chunks):
            rows = slice(c * L, (c + 1) * L)
            upd.append(_dot_tn(vb_s[rows, cols], kdec_s[rows, cols]))
        st = None
        for c in range(n_chunks):
            rows = slice(c * L, (c + 1) * L)
            if stateful or c == 0:
                st = lin_state_in(c, h)
            o_inter.append(_dot_nt(qt_s[rows, cols], st.astype(BF16)))
            st = st * dec_last[c][:, cols] + upd[c]
            if stateful or c == last_c:
                lin_state_out(c, h, st)
        for c in range(n_chunks):
            rows = slice(c * L, (c + 1) * L)
            o = o_inter[c] + _dot(att[c], vb_s[rows, cols])
            o = o * lax.rsqrt(jnp.mean(o * o, axis=-1, keepdims=True) + EPS)
            o_s[rows, cols] = (o * gn[:, cols] * g_s[rows, cols]).astype(BF16)

    li_all = pltpu.roll(gates_all, H_C, axis=1)
    lane_l = lax.broadcasted_iota(jnp.int32, (L, LANE), 1)
    pad = LANE - L

    def transpose_rows(a):
        if pad:
            return jnp.concatenate([a, jnp.zeros((pad, LANE), F32)], axis=0).T[:, :L]
        return a.T

    head_cols = [slice(h * DK_C, (h + 1) * DK_C) for h in range(H_C)]
    head_kcols = [slice(W_C + h * DK_C, W_C + (h + 1) * DK_C) for h in range(H_C)]
    chunk_rows = [slice(c * L, (c + 1) * L) for c in range(n_chunks)]
    qkp = [[None] * H_C for _ in range(n_chunks)]
    upd_c = [[None] * H_C for _ in range(n_chunks)]
    upd_n = [[None] * H_C for _ in range(n_chunks)]
    o_intra = [[None] * H_C for _ in range(n_chunks)]
    dmax, rsum = [], []
    for c in range(n_chunks):
        rows = chunk_rows[c]
        gcum = gcum_all[rows, :]
        gcum_t = transpose_rows(gcum)
        li_t = transpose_rows(li_all[rows, :])
        dm_pack = jnp.zeros((L, LANE), F32)
        rs_pack = jnp.zeros((L, LANE), F32)
        for h in range(H_C):
            j = H_C + h
            dmat = jnp.where(tril, gcum[:, j:j + 1] - gcum_t[j:j + 1, :] + li_t[j:j + 1, :], -jnp.inf)
            dm = jnp.max(dmat, axis=1, keepdims=True)
            qh = qkp_s[rows, head_cols[h]].astype(BF16)
            kh = qkp_s[rows, head_kcols[h]].astype(BF16)
            p = _dot_nt(qh, kh) * jnp.exp(dmat - dm)
            qkp[c][h] = p.astype(BF16)
            dm_pack = jnp.where(lane_l == j, dm, dm_pack)
            rs_pack = jnp.where(lane_l == j, jnp.sum(p, axis=-1, keepdims=True), rs_pack)
        dmax.append(dm_pack)
        rsum.append(rs_pack)
    for c in range(n_chunks):
        rows = chunk_rows[c]
        gcum = gcum_all[rows, :]
        w_l = jnp.exp(gcum[L - 1:L, :] - gcum + li_all[rows, :] - dmax[c][L - 1:L, :])
        for h in range(H_C):
            j = H_C + h
            kw = qkp_s[rows, head_kcols[h]] * w_l[:, j:j + 1]
            vh = vc_s[rows, head_cols[h]]
            upd_c[c][h] = _dot_tn(kw.astype(BF16), vh)
            upd_n[c][h] = jnp.sum(kw, axis=0, keepdims=True)
            o_intra[c][h] = _dot(qkp[c][h], vh)
    gnm = gnm_ref[...]
    cst = [None] * H_C
    nrow = [None] * H_C
    m_row = None
    for c in range(n_chunks):
        rows = chunk_rows[c]
        if stateful:
            m_row = sm_in[c, 0:1, :]
        elif c == 0:
            m_row = m_s[0:1, :]
        qn_pack = jnp.zeros((L, LANE), F32)
        for h in range(H_C):
            if stateful:
                cst[h], nrow[h] = sc_in[c, h], sn_in[c, h:h + 1, :]
            elif c == 0:
                cst[h], nrow[h] = c_s[h], n_s[h:h + 1, :]
            qn = jnp.sum(qkp_s[rows, head_cols[h]] * nrow[h], axis=-1, keepdims=True)
            qn_pack = jnp.where(lane_l == H_C + h, qn, qn_pack)
        gcum = gcum_all[rows, :]
        w0 = gcum + m_row
        m_t = jnp.maximum(w0, dmax[c])
        s0 = jnp.exp(w0 - m_t)
        r = jnp.exp(dmax[c] - m_t)
        den = s0 * qn_pack + r * rsum[c]
        inv = 1.0 / jnp.maximum(jnp.abs(den), jnp.exp(-m_t))
        a_inter = s0 * inv
        a_intra = r * inv
        m_last = m_t[L - 1:L, :]
        s_l = jnp.exp(gcum[L - 1:L, :] + m_row - m_last)
        r_l = r[L - 1:L, :]
        m_row = m_last
        for h in range(H_C):
            j = H_C + h
            qh = qkp_s[rows, head_cols[h]].astype(BF16)
            hh = (a_inter[:, j:j + 1] * _dot(qh, cst[h].astype(BF16))
                  + a_intra[:, j:j + 1] * o_intra[c][h])
            cst[h] = s_l[:, j:j + 1] * cst[h] + r_l[:, j:j + 1] * upd_c[c][h]
            nrow[h] = s_l[:, j:j + 1] * nrow[h] + r_l[:, j:j + 1] * upd_n[c][h]
            if stateful:
                sc_out[c, h] = cst[h]
                sn_out[c, h:h + 1, :] = nrow[h]
            elif c == last_c:
                c_s[h] = cst[h]
                n_s[h:h + 1, :] = nrow[h]
            hh = hh * lax.rsqrt(jnp.mean(hh * hh, axis=-1, keepdims=True) + EPS)
            o_s[rows, W_LIN + h * DV_C:W_LIN + (h + 1) * DV_C] = (
                hh * gnm[:, head_cols[h]] * oc_s[rows, head_cols[h]]).astype(BF16)
        if stateful:
            sm_out[c] = jnp.broadcast_to(m_last, (SUBLANE, LANE))

    if not stateful:
        m_s[...] = jnp.broadcast_to(m_row, (SUBLANE, LANE))

    hb = h_s[...]
    merged = None
    for n in range(N_BRANCH):
        proj = _dot(o_s[:, n * BRANCH_WIDTH:(n + 1) * BRANCH_WIDTH], wbr_ref[n])
        gm = seg(O_GM + n * D_MODEL, D_MODEL)
        term = _sigmoid(gm) * proj
        merged = term if merged is None else merged + term
    x1_ref[...] = x_ref[...] + _dot(merged.astype(BF16), wout_ref[...])

    if not stateful:
        @pl.when(t_idx == n_tblocks - 1)
        def _():
            for h in range(H_A):
                sgla_out[0, h] = st_s[h].T[:DK_A, :]
            for h in range(H_B):
                shgrn_out[0, h] = st_s[H_A + h].T
            sc_out[0] = c_s[...]
            sn_out[0] = n_s[0:H_C, :]
            sm_out[0] = m_s[...]
            scc_out[0] = conv_s[...]


def _const_spec(shape):
    nd = len(shape)
    return pl.BlockSpec(shape, lambda i, j, _nd=nd: (0,) * _nd)


def _mixer_call(x2d, states, params, *, n_streams, t_len, streams_per_step, tt):
    stateful = states is not None
    if stateful:
        L = t_len
        assert L % SUBLANE == 0 and L <= CHUNK
        n_chunks = streams_per_step
        tt = streams_per_step * L
        n_tblocks = 1
        grid = (n_streams // streams_per_step, 1)
        ns = streams_per_step
    else:
        L = CHUNK
        assert t_len % tt == 0 and tt % L == 0
        n_chunks = tt // L
        n_tblocks = t_len // tt
        grid = (n_streams, n_tblocks)
        ns = 1

    x_spec = pl.BlockSpec((tt, D_MODEL), lambda i, j: (i * n_tblocks + j, 0))
    state_shapes = [
        (n_streams, H_A, DK_A, DV_A), (n_streams, H_B, DK_B, DV_B), (n_streams, H_C, DK_C, DV_C),
        (n_streams, H_C, DK_C), (n_streams, SUBLANE, LANE), (n_streams, SUBLANE, W_QKC),
    ]

    def state_spec(shape):
        nd = len(shape)
        return pl.BlockSpec((ns,) + shape[1:], lambda i, j, _nd=nd: (i,) + (0,) * (_nd - 1))

    in_specs = [x_spec]
    args = [x2d]
    if stateful:
        in_specs += [state_spec(s) for s in state_shapes]
        args += list(states)
    in_specs += [_const_spec(p.shape) for p in params]
    args += list(params)

    out_shape = [jax.ShapeDtypeStruct(x2d.shape, F32)] + [jax.ShapeDtypeStruct(s, F32) for s in state_shapes]
    out_specs = [x_spec] + [state_spec(s) for s in state_shapes]

    scratch = [
        pltpu.VMEM((tt, D_MODEL), BF16),
        pltpu.VMEM((tt, W_LIN), F32),
        pltpu.VMEM((tt, W_LIN), F32),
        pltpu.VMEM((tt, W_LIN), F32),
        pltpu.VMEM((tt, W_QKC), F32),
        pltpu.VMEM((tt, W_C), BF16),
        pltpu.VMEM((tt, W_C), F32),
        pltpu.VMEM((tt, N_BRANCH * BRANCH_WIDTH), BF16),
        pltpu.VMEM((tt, W_LIN), BF16),
        pltpu.VMEM((tt, W_LIN), BF16),
        pltpu.VMEM((tt, W_LIN), BF16),
        pltpu.VMEM((tt, W_LIN), BF16),
        pltpu.VMEM((N_LIN, HP, HP), F32),
        pltpu.VMEM((H_C, DK_C, DV_C), F32),
        pltpu.VMEM((SUBLANE, LANE), F32),
        pltpu.VMEM((SUBLANE, LANE), F32),
        pltpu.VMEM((SUBLANE, W_QKC), F32),
        pltpu.VMEM((SUBLANE + (L if stateful else tt), W_QKC), F32),
    ]
    kern = functools.partial(_mixer_kernel, L=L, n_chunks=n_chunks, n_tblocks=n_tblocks, stateful=stateful)
    return pl.pallas_call(
        kern,
        grid=grid,
        in_specs=in_specs,
        out_specs=out_specs,
        out_shape=out_shape,
        scratch_shapes=scratch,
        compiler_params=pltpu.CompilerParams(
            dimension_semantics=("arbitrary", "arbitrary"), vmem_limit_bytes=VMEM_LIMIT_BYTES),
        name="mixer_sample" if stateful else "mixer_prompt",
    )(*args)


def _ffn_kernel(*refs, n_sub, ls, n_tblocks, stateful, final):
    it = iter(refs)
    x_ref = next(it)
    if stateful:
        scf_in = next(it)
    g2_ref, wup_ref, convf_ref, wdown_ref, gfin_ref = (next(it) for _ in range(5))
    y_ref, scf_out = next(it), next(it)
    hist_s, cbuf_s, act_s = next(it), next(it), next(it)

    t_idx = pl.program_id(1)
    if not stateful:
        @pl.when(t_idx == 0)
        def _():
            hist_s[...] = jnp.zeros_like(hist_s)

    x = x_ref[...]
    hb = _rmsnorm_rows(x, g2_ref[...]).astype(BF16)
    ug = _dot(hb, wup_ref[:, :D_FF])
    uv = _dot(hb, wup_ref[:, D_FF:])

    wconv = convf_ref[...]
    for s in range(n_sub):
        r0 = s * ls
        cbuf_s[0:SUBLANE, :] = scf_in[s] if stateful else hist_s[...]
        cbuf_s[SUBLANE:SUBLANE + ls, :] = ug[r0:r0 + ls, :]
        new_hist = cbuf_s[ls:ls + SUBLANE, :]
        if stateful:
            scf_out[s] = new_hist
        else:
            hist_s[...] = new_hist
        acc = cbuf_s[CONV_F_ROW0:CONV_F_ROW0 + ls, :] * wconv[0:1, :]
        for j in range(1, CONV_F):
            acc = acc + cbuf_s[CONV_F_ROW0 + j:CONV_F_ROW0 + j + ls, :] * wconv[j:j + 1, :]
        gel = 0.5 * acc * (1.0 + lax.erf(acc * (2.0 ** -0.5)))
        act_s[r0:r0 + ls, :] = (gel * uv[r0:r0 + ls, :]).astype(BF16)

    x2 = x + _dot(act_s[...], wdown_ref[...])
    if final:
        x2 = _rmsnorm_rows(x2, gfin_ref[...])
    y_ref[...] = x2

    if not stateful:
        @pl.when(t_idx == n_tblocks - 1)
        def _():
            scf_out[0] = hist_s[...]


def _ffn_call(x2d, state, params, *, n_streams, t_len, streams_per_step, tt, final):
    stateful = state is not None
    if stateful:
        ls = t_len
        n_sub = streams_per_step
        tt = n_sub * ls
        n_tblocks = 1
        grid = (n_streams // n_sub, 1)
    else:
        assert t_len % tt == 0
        ls = tt
        n_sub = 1
        n_tblocks = t_len // tt
        grid = (n_streams, n_tblocks)

    x_spec = pl.BlockSpec((tt, D_MODEL), lambda i, j: (i * n_tblocks + j, 0))
    st_shape = (n_streams, SUBLANE, D_FF)
    st_spec = pl.BlockSpec((n_sub, SUBLANE, D_FF), lambda i, j: (i, 0, 0))
    in_specs = [x_spec]
    args = [x2d]
    if stateful:
        in_specs.append(st_spec)
        args.append(state)
    in_specs += [_const_spec(p.shape) for p in params]
    args += list(params)
    scratch = [
        pltpu.VMEM((SUBLANE, D_FF), F32),
        pltpu.VMEM((SUBLANE + ls, D_FF), F32),
        pltpu.VMEM((tt, D_FF), BF16),
    ]
    kern = functools.partial(_ffn_kernel, n_sub=n_sub, ls=ls, n_tblocks=n_tblocks, stateful=stateful, final=final)
    return pl.pallas_call(
        kern,
        grid=grid,
        in_specs=in_specs,
        out_specs=[x_spec, st_spec],
        out_shape=[jax.ShapeDtypeStruct(x2d.shape, F32), jax.ShapeDtypeStruct(st_shape, F32)],
        scratch_shapes=scratch,
        compiler_params=pltpu.CompilerParams(
            dimension_semantics=("arbitrary", "arbitrary"), vmem_limit_bytes=VMEM_LIMIT_BYTES),
        name="ffn_sample" if stateful else "ffn_prompt",
    )(*args)


def _pad_heads(a):
    r = a.shape[0]
    a = a.reshape(r, H_A, DK_A)
    return jnp.pad(a, ((0, 0), (0, 0), (0, HP - DK_A))).reshape(r, H_A * HP)


def _pack_cols(w):
    qa, ka, va, ga, ra, qb, fb, ib, gb, qkc, vc, oc, ic, fc, gm = jnp.split(w, SPLIT_IDX, axis=-1)
    ra_p = jnp.pad(ra, ((0, 0), (0, LANE - GLA_RANK)))
    if_p = jnp.pad(jnp.concatenate([ic, fc], axis=-1), ((0, 0), (0, LANE - 2 * H_C)))
    return jnp.concatenate(
        [_pad_heads(qa), qb, _pad_heads(ka), fb, va, ib, ga, gb, ra_p, qkc, vc, oc, if_p, gm], axis=-1)


def _layer_params(l, lower, norm1_g, w_in, b_in, w_gla_gate, b_gla_gate, gn_gla, gn_hgrn, conv_mlstm,
                  b_mlstm_f, gn_mlstm, w_branch, w_out, norm2_g, w_up, conv_ffn, w_down, final_g):
    lb = lower[l][None, :]
    wg = jnp.pad(_pad_heads(w_gla_gate[l]), ((0, LANE - GLA_RANK), (0, 0))).astype(BF16)
    bfrow = jnp.pad(b_mlstm_f[l][None, :], ((0, 0), (H_C, LANE - 2 * H_C)))
    mixer = (
        norm1_g[l][None, :], _pack_cols(w_in[l]).astype(BF16), _pack_cols(b_in[l][None, :]),
        wg, _pad_heads(b_gla_gate[l][None, :]),
        jnp.concatenate([gn_gla[l], gn_hgrn[l]])[None, :],
        jnp.log(lb), jnp.log1p(-lb), 1.0 - lb,
        conv_mlstm[l], bfrow, gn_mlstm[l][None, :],
        w_branch[l].astype(BF16), w_out[l].astype(BF16),
    )
    ffn = (norm2_g[l][None, :], w_up[l].astype(BF16), conv_ffn[l], w_down[l].astype(BF16), final_g[None, :])
    return mixer, ffn


def _run_group(x, states, layer_params, *, streams_per_step, tt_mixer, tt_ffn):
    n_streams, t_len, _ = x.shape
    x2d = x.reshape(n_streams * t_len, D_MODEL)
    new_states = []
    for l in range(DEPTH):
        mixer_p, ffn_p = layer_params[l]
        if states is None:
            mix_in, ffn_in = None, None
        else:
            s_gla, s_hgrn, s_c, s_n, s_m, s_cc, s_cf = (s[l] for s in states)
            m_tile = jnp.broadcast_to(jnp.pad(s_m, ((0, 0), (H_C, LANE - 2 * H_C)))[:, None, :],
                                      (n_streams, SUBLANE, LANE))
            cc_tile = jnp.pad(s_cc, ((0, 0), (CONV_C_ROW0, 0), (0, 0)))
            mix_in = (s_gla, s_hgrn, s_c, s_n, m_tile, cc_tile)
            ffn_in = jnp.pad(s_cf, ((0, 0), (CONV_F_ROW0, 0), (0, 0)))
        x2d, o_gla, o_hgrn, o_c, o_n, o_m, o_cc = _mixer_call(
            x2d, mix_in, mixer_p, n_streams=n_streams, t_len=t_len,
            streams_per_step=streams_per_step, tt=tt_mixer)
        x2d, o_cf = _ffn_call(
            x2d, ffn_in, ffn_p, n_streams=n_streams, t_len=t_len,
            streams_per_step=streams_per_step, tt=tt_ffn, final=(l == DEPTH - 1))
        new_states.append((o_gla, o_hgrn, o_c, o_n, o_m[:, 0, H_C:2 * H_C], o_cc[:, CONV_C_ROW0:, :],
                           o_cf[:, CONV_F_ROW0:, :]))
    stacked = [jnp.stack([ns[i] for ns in new_states], axis=0) for i in range(7)]
    return x2d.reshape(n_streams, t_len, D_MODEL), stacked


def kernel(x_prompt, x_sample, state_gla, state_hgrn, state_mlstm_c, state_mlstm_n, state_mlstm_m,
           state_mlstm_conv, state_ffn_conv, norm1_g, w_in, b_in, w_gla_gate, b_gla_gate, gn_gla, lb_logits,
           gn_hgrn, conv_mlstm, b_mlstm_f, gn_mlstm, w_branch, w_out, norm2_g, w_up, conv_ffn, w_down, final_g):
    lb_cum = jnp.cumsum(jax.nn.softmax(lb_logits.astype(F32), axis=0), axis=0)
    lower = lb_cum - lb_cum[0]
    layer_params = [
        _layer_params(l, lower, norm1_g, w_in, b_in, w_gla_gate, b_gla_gate, gn_gla, gn_hgrn, conv_mlstm,
                      b_mlstm_f, gn_mlstm, w_branch, w_out, norm2_g, w_up, conv_ffn, w_down, final_g)
        for l in range(DEPTH)
    ]
    y_p, p_states = _run_group(x_prompt, None, layer_params, streams_per_step=1, tt_mixer=256, tt_ffn=256)
    sample_states = (state_gla, state_hgrn, state_mlstm_c, state_mlstm_n, state_mlstm_m,
                     state_mlstm_conv, state_ffn_conv)
    y_s, s_states = _run_group(x_sample, sample_states, layer_params, streams_per_step=4,
                               tt_mixer=None, tt_ffn=None)
    return (y_p, y_s, *p_states, *s_states)
```

```python
import functools

import jax
import jax.numpy as jnp
from jax import lax
from jax.experimental import pallas as pl
from jax.experimental.pallas import tpu as pltpu

F32 = jnp.float32
BF16 = jnp.bfloat16

D_MODEL = 1024
DEPTH = 2
CHUNK = 64
BRANCH_WIDTH = D_MODEL // 2
H_A, DV_A = 4, BRANCH_WIDTH // 4
DK_A = DV_A // 2
GLA_RANK = 16
GLA_TAU = 16.0
H_B = 4
DK_B = DV_B = BRANCH_WIDTH // H_B
H_C = 4
DK_C = DV_C = BRANCH_WIDTH // H_C
CONV_C = 4
N_BRANCH = 3
D_FF = ((8 * D_MODEL // 3 + 127) // 128) * 128
CONV_F = 3
EPS = 1e-6

SPLIT_SIZES = (
    H_A * DK_A, H_A * DK_A, H_A * DV_A, H_A * DV_A, GLA_RANK,
    H_B * DK_B, H_B * DK_B, H_B * DV_B, H_B * DV_B,
    2 * H_C * DK_C, H_C * DV_C, H_C * DV_C, H_C, H_C,
    N_BRANCH * D_MODEL,
)
SPLIT_IDX = tuple(int(sum(SPLIT_SIZES[:i + 1])) for i in range(len(SPLIT_SIZES) - 1))

LANE = 128
SUBLANE = 8
VMEM_LIMIT_BYTES = 60 * 1024 * 1024
MAX_FACTORED_DECAY = 80.0

HP = LANE
N_LIN = H_A + H_B
W_LIN = N_LIN * HP
O_Q = 0
O_K = O_Q + W_LIN
O_V = O_K + W_LIN
O_G = O_V + W_LIN
O_RA = O_G + W_LIN
O_IF = O_RA + LANE
O_QKC = O_IF + LANE
O_VC = O_QKC + 2 * H_C * DK_C
O_OC = O_VC + H_C * DV_C
O_GM = O_OC + H_C * DV_C
N_PACK = O_GM + N_BRANCH * D_MODEL
W_QKC = 2 * H_C * DK_C
W_C = H_C * DV_C
CONV_C_ROW0 = SUBLANE - (CONV_C - 1)
CONV_F_ROW0 = SUBLANE - (CONV_F - 1)


def _sigmoid(x):
    return 1.0 / (1.0 + jnp.exp(-x))


def _silu(x):
    return x * _sigmoid(x)


def _log_sigmoid(x):
    return jnp.minimum(x, 0.0) - jnp.log(1.0 + jnp.exp(-jnp.abs(x)))


def _rmsnorm_rows(x, g):
    return x * lax.rsqrt(jnp.mean(x * x, axis=-1, keepdims=True) + EPS) * g


def _dot(a, b):
    return jnp.dot(a, b, preferred_element_type=F32)


def _dot_nt(a, b):
    return lax.dot_general(a, b, (((1,), (1,)), ((), ())), preferred_element_type=F32)


def _dot_tn(a, b):
    return lax.dot_general(a, b, (((0,), (0,)), ((), ())), preferred_element_type=F32)


def _chunk_cumsum(a, chunk):
    rows, width = a.shape
    assert chunk % SUBLANE == 0 and rows % chunk == 0
    row_in_tile = lax.broadcasted_iota(jnp.int32, a.shape, 0) & (SUBLANE - 1)
    shift = 1
    while shift < SUBLANE:
        a = a + jnp.where(row_in_tile >= shift, pltpu.roll(a, shift, axis=0), 0.0)
        shift *= 2
    a3 = a.reshape(rows // chunk, chunk, width)
    tiles = [a3[:, 0:SUBLANE, :]]
    for i in range(1, chunk // SUBLANE):
        tiles.append(a3[:, i * SUBLANE:(i + 1) * SUBLANE, :] + tiles[-1][:, SUBLANE - 1:SUBLANE, :])
    return jnp.concatenate(tiles, axis=1).reshape(rows, width)


def _mixer_kernel(*refs, L, n_chunks, n_tblocks, stateful, pairwise):
    it = iter(refs)
    x_ref = next(it)
    if stateful:
        sgla_in, shgrn_in, sc_in, sn_in, sm_in, scc_in = (next(it) for _ in range(6))
    (g1_ref, win_ref, bin_ref, wg_ref, bg_ref, gnlin_ref, loglb_ref, log1mlb_ref, onemlb_ref,
     convm_ref, bf_ref, gnm_ref, wbr_ref, wout_ref) = (next(it) for _ in range(14))
    x1_ref, sgla_out, shgrn_out, sc_out, sn_out, sm_out, scc_out, dmin_out = (next(it) for _ in range(8))
    (h_s, k_s, la_s, g_s, qkp_s, vc_s, oc_s, o_s, qt_s, kt_s, kdec_s, vb_s,
     st_s, c_s, n_s, m_s, conv_s, cbuf_s) = (next(it) for _ in range(18))
    if pairwise:
        att_s, qf_s = next(it), next(it)

    t_idx = pl.program_id(1)
    tt = n_chunks * L
    last_c = n_chunks - 1
    wa = H_A * HP

    if not stateful:
        @pl.when(t_idx == 0)
        def _():
            st_s[...] = jnp.zeros_like(st_s)
            c_s[...] = jnp.zeros_like(c_s)
            n_s[...] = jnp.zeros_like(n_s)
            m_s[...] = jnp.zeros_like(m_s)
            conv_s[...] = jnp.zeros_like(conv_s)

    row_i = lax.broadcasted_iota(jnp.int32, (L, L), 0)
    col_i = lax.broadcasted_iota(jnp.int32, (L, L), 1)
    tril = row_i >= col_i

    x = x_ref[...]
    hb = _rmsnorm_rows(x, g1_ref[...]).astype(BF16)
    h_s[...] = hb

    def seg(off, width):
        return _dot(hb, win_ref[:, off:off + width]) + bin_ref[:, off:off + width]

    def lin_q():
        zq = seg(O_Q, W_LIN)
        return zq[:, :wa] * (DK_A ** -0.5), _silu(zq[:, wa:])

    z_small = seg(O_RA, 2 * LANE)
    ra = z_small[:, :LANE].astype(BF16)
    zif = z_small[:, LANE:]
    la_s[:, :wa] = _log_sigmoid(_dot(ra, wg_ref[...]) + bg_ref[...]) * (1.0 / GLA_TAU)
    zk = seg(O_K, W_LIN)
    k_s[:, :wa] = zk[:, :wa]
    fb = zk[:, wa:]
    u = jnp.exp(-jnp.abs(fb))
    log_sig = jnp.minimum(fb, 0.0) - jnp.log(1.0 + u)
    a_lb = loglb_ref[...]
    c_lb = log1mlb_ref[...] + log_sig
    la_s[:, wa:] = jnp.maximum(a_lb, c_lb) + jnp.log(1.0 + jnp.exp(-jnp.abs(a_lb - c_lb)))
    k_s[:, wa:] = onemlb_ref[...] * (jnp.where(fb >= 0.0, u, 1.0) / (1.0 + u))

    b = _chunk_cumsum(la_s[...], L)
    b3 = b.reshape(n_chunks, L, W_LIN)
    b_last = b3[:, L - 1:L, :]
    dec_last = jnp.exp(b_last)
    q_gla, q_hgrn = lin_q()
    eb = jnp.exp(b)
    qt_s[:, :wa] = (q_gla * eb[:, :wa]).astype(BF16)
    qt_s[:, wa:] = (q_hgrn * eb[:, wa:]).astype(BF16)
    kk = k_s[...]
    kt_s[...] = (kk * jnp.exp(-b)).astype(BF16)
    kdec_s[...] = (kk.reshape(n_chunks, L, W_LIN) * jnp.exp(b_last - b3)).reshape(tt, W_LIN).astype(BF16)
    vb_s[...] = seg(O_V, W_LIN).astype(BF16)
    g_s[...] = _silu(seg(O_G, W_LIN))
    qkp_s[...] = seg(O_QKC, W_QKC)
    vc_s[...] = seg(O_VC, W_C).astype(BF16)
    oc_s[...] = _sigmoid(seg(O_OC, W_C))
    lane = lax.broadcasted_iota(jnp.int32, zif.shape, 1)
    gates_all = jnp.where(lane >= H_C, _log_sigmoid(zif + bf_ref[...]), zif)
    gcum_all = _chunk_cumsum(gates_all, L)

    n_sub = n_chunks if stateful else 1
    ls = tt // n_sub
    wconv = convm_ref[...]
    for s in range(n_sub):
        r0 = s * ls
        cbuf_s[0:SUBLANE, :] = scc_in[s] if stateful else conv_s[...]
        cbuf_s[SUBLANE:SUBLANE + ls, :] = qkp_s[r0:r0 + ls, :]
        new_hist = cbuf_s[ls:ls + SUBLANE, :]
        if stateful:
            scc_out[s] = new_hist
        else:
            conv_s[...] = new_hist
        acc = cbuf_s[CONV_C_ROW0:CONV_C_ROW0 + ls, :] * wconv[0:1, :]
        for j in range(1, CONV_C):
            acc = acc + cbuf_s[CONV_C_ROW0 + j:CONV_C_ROW0 + j + ls, :] * wconv[j:j + 1, :]
        qk = _silu(acc)
        qkp_s[r0:r0 + ls, :W_C] = qk[:, :W_C]
        qkp_s[r0:r0 + ls, W_C:] = qk[:, W_C:] * (DK_C ** -0.5)

    dmin_out[...] = jnp.broadcast_to(jnp.min(b_last), dmin_out.shape)
    if pairwise:
        qf_s[:, :wa] = q_gla
        qf_s[:, wa:] = q_hgrn
        la_s[...] = b
        lane_l0 = lax.broadcasted_iota(jnp.int32, (L, LANE), 1)

        def per_chunk(c, carry):
            r0 = pl.multiple_of(c * L, L)
            for h in range(N_LIN):
                cols = slice(h * HP, (h + 1) * HP)
                qh = qf_s[pl.ds(r0, L), cols]
                bh = la_s[pl.ds(r0, L), cols]

                def per_source_group(g, acc):
                    s0 = pl.multiple_of(g * SUBLANE, SUBLANE)
                    k8 = k_s[pl.ds(r0 + s0, SUBLANE), cols]
                    b8 = la_s[pl.ds(r0 + s0, SUBLANE), cols]
                    for j in range(SUBLANE):
                        e = jnp.exp(jnp.minimum(bh - b8[j:j + 1, :], 0.0))
                        col = jnp.sum(qh * k8[j:j + 1, :] * e, axis=-1, keepdims=True)
                        acc = jnp.where(lane_l0 == s0 + j, col, acc)
                    return acc

                acc = lax.fori_loop(0, L // SUBLANE, per_source_group, jnp.zeros((L, LANE), F32))
                att_s[c * N_LIN + h] = jnp.where(tril, acc[:, :L], 0.0).astype(BF16)
            return carry

        lax.fori_loop(0, n_chunks, per_chunk, 0)

    def lin_state_in(c, h):
        if not stateful:
            return st_s[h]
        if h < H_A:
            return jnp.concatenate([sgla_in[c, h], jnp.zeros((HP - DK_A, DV_A), F32)], axis=0).T
        return shgrn_in[c, h - H_A].T

    def lin_state_out(c, h, s):
        if not stateful:
            st_s[h] = s
        elif h < H_A:
            sgla_out[c, h] = s.T[:DK_A, :]
        else:
            shgrn_out[c, h - H_A] = s.T

    gn = gnlin_ref[...]
    for h in range(N_LIN):
        cols = slice(h * HP, (h + 1) * HP)
        att, upd, o_inter = [], [], []
        for c in range(n_chunks):
            rows = slice(c * L, (c + 1) * L)
            if pairwise:
                att.append(att_s[c * N_LIN + h])
            else:
                a = _dot_nt(qt_s[rows, cols], kt_s[rows, cols])
                att.append(jnp.where(tril, a, 0.0).astype(BF16))
        for c in range(n_chunks):
            rows = slice(c * L, (c + 1) * L)
            upd.append(_dot_tn(vb_s[rows, cols], kdec_s[rows, cols]))
        st = None
        for c in range(n_chunks):
            rows = slice(c * L, (c + 1) * L)
            if stateful or c == 0:
                st = lin_state_in(c, h)
            o_inter.append(_dot_nt(qt_s[rows, cols], st.astype(BF16)))
            st = st * dec_last[c][:, cols] + upd[c]
            if stateful or c == last_c:
                lin_state_out(c, h, st)
        for c in range(n_chunks):
            rows = slice(c * L, (c + 1) * L)
            o = o_inter[c] + _dot(att[c], vb_s[rows, cols])
            o = o * lax.rsqrt(jnp.mean(o * o, axis=-1, keepdims=True) + EPS)
            o_s[rows, cols] = (o * gn[:, cols] * g_s[rows, cols]).astype(BF16)

    li_all = pltpu.roll(gates_all, H_C, axis=1)
    lane_l = lax.broadcasted_iota(jnp.int32, (L, LANE), 1)
    pad = LANE - L

    def transpose_rows(a):
        if pad:
            return jnp.concatenate([a, jnp.zeros((pad, LANE), F32)], axis=0).T[:, :L]
        return a.T

    head_cols = [slice(h * DK_C, (h + 1) * DK_C) for h in range(H_C)]
    head_kcols = [slice(W_C + h * DK_C, W_C + (h + 1) * DK_C) for h in range(H_C)]
    chunk_rows = [slice(c * L, (c + 1) * L) for c in range(n_chunks)]
    qkp = [[None] * H_C for _ in range(n_chunks)]
    upd_c = [[None] * H_C for _ in range(n_chunks)]
    upd_n = [[None] * H_C for _ in range(n_chunks)]
    o_intra = [[None] * H_C for _ in range(n_chunks)]
    dmax, rsum = [], []
    for c in range(n_chunks):
        rows = chunk_rows[c]
        gcum = gcum_all[rows, :]
        gcum_t = transpose_rows(gcum)
        li_t = transpose_rows(li_all[rows, :])
        dm_pack = jnp.zeros((L, LANE), F32)
        rs_pack = jnp.zeros((L, LANE), F32)
        for h in range(H_C):
            j = H_C + h
            dmat = jnp.where(tril, gcum[:, j:j + 1] - gcum_t[j:j + 1, :] + li_t[j:j + 1, :], -jnp.inf)
            dm = jnp.max(dmat, axis=1, keepdims=True)
            qh = qkp_s[rows, head_cols[h]].astype(BF16)
            kh = qkp_s[rows, head_kcols[h]].astype(BF16)
            p = _dot_nt(qh, kh) * jnp.exp(dmat - dm)
            qkp[c][h] = p.astype(BF16)
            dm_pack = jnp.where(lane_l == j, dm, dm_pack)
            rs_pack = jnp.where(lane_l == j, jnp.sum(p, axis=-1, keepdims=True), rs_pack)
        dmax.append(dm_pack)
        rsum.append(rs_pack)
    for c in range(n_chunks):
        rows = chunk_rows[c]
        gcum = gcum_all[rows, :]
        w_l = jnp.exp(gcum[L - 1:L, :] - gcum + li_all[rows, :] - dmax[c][L - 1:L, :])
        for h in range(H_C):
            j = H_C + h
            kw = qkp_s[rows, head_kcols[h]] * w_l[:, j:j + 1]
            vh = vc_s[rows, head_cols[h]]
            upd_c[c][h] = _dot_tn(kw.astype(BF16), vh)
            upd_n[c][h] = jnp.sum(kw, axis=0, keepdims=True)
            o_intra[c][h] = _dot(qkp[c][h], vh)
    gnm = gnm_ref[...]
    cst = [None] * H_C
    nrow = [None] * H_C
    m_row = None
    for c in range(n_chunks):
        rows = chunk_rows[c]
        if stateful:
            m_row = sm_in[c, 0:1, :]
        elif c == 0:
            m_row = m_s[0:1, :]
        qn_pack = jnp.zeros((L, LANE), F32)
        for h in range(H_C):
            if stateful:
                cst[h], nrow[h] = sc_in[c, h], sn_in[c, h:h + 1, :]
            elif c == 0:
                cst[h], nrow[h] = c_s[h], n_s[h:h + 1, :]
            qn = jnp.sum(qkp_s[rows, head_cols[h]] * nrow[h], axis=-1, keepdims=True)
            qn_pack = jnp.where(lane_l == H_C + h, qn, qn_pack)
        gcum = gcum_all[rows, :]
        w0 = gcum + m_row
        m_t = jnp.maximum(w0, dmax[c])
        s0 = jnp.exp(w0 - m_t)
        r = jnp.exp(dmax[c] - m_t)
        den = s0 * qn_pack + r * rsum[c]
        inv = 1.0 / jnp.maximum(jnp.abs(den), jnp.exp(-m_t))
        a_inter = s0 * inv
        a_intra = r * inv
        m_last = m_t[L - 1:L, :]
        s_l = jnp.exp(gcum[L - 1:L, :] + m_row - m_last)
        r_l = r[L - 1:L, :]
        m_row = m_last
        for h in range(H_C):
            j = H_C + h
            qh = qkp_s[rows, head_cols[h]].astype(BF16)
            hh = (a_inter[:, j:j + 1] * _dot(qh, cst[h].astype(BF16))
                  + a_intra[:, j:j + 1] * o_intra[c][h])
            cst[h] = s_l[:, j:j + 1] * cst[h] + r_l[:, j:j + 1] * upd_c[c][h]
            nrow[h] = s_l[:, j:j + 1] * nrow[h] + r_l[:, j:j + 1] * upd_n[c][h]
            if stateful:
                sc_out[c, h] = cst[h]
                sn_out[c, h:h + 1, :] = nrow[h]
            elif c == last_c:
                c_s[h] = cst[h]
                n_s[h:h + 1, :] = nrow[h]
            hh = hh * lax.rsqrt(jnp.mean(hh * hh, axis=-1, keepdims=True) + EPS)
            o_s[rows, W_LIN + h * DV_C:W_LIN + (h + 1) * DV_C] = (
                hh * gnm[:, head_cols[h]] * oc_s[rows, head_cols[h]]).astype(BF16)
        if stateful:
            sm_out[c] = jnp.broadcast_to(m_last, (SUBLANE, LANE))

    if not stateful:
        m_s[...] = jnp.broadcast_to(m_row, (SUBLANE, LANE))

    hb = h_s[...]
    merged = None
    for n in range(N_BRANCH):
        proj = _dot(o_s[:, n * BRANCH_WIDTH:(n + 1) * BRANCH_WIDTH], wbr_ref[n])
        gm = seg(O_GM + n * D_MODEL, D_MODEL)
        term = _sigmoid(gm) * proj
        merged = term if merged is None else merged + term
    x1_ref[...] = x_ref[...] + _dot(merged.astype(BF16), wout_ref[...])

    if not stateful:
        @pl.when(t_idx == n_tblocks - 1)
        def _():
            for h in range(H_A):
                sgla_out[0, h] = st_s[h].T[:DK_A, :]
            for h in range(H_B):
                shgrn_out[0, h] = st_s[H_A + h].T
            sc_out[0] = c_s[...]
            sn_out[0] = n_s[0:H_C, :]
            sm_out[0] = m_s[...]
            scc_out[0] = conv_s[...]


def _const_spec(shape):
    nd = len(shape)
    return pl.BlockSpec(shape, lambda i, j, _nd=nd: (0,) * _nd)


def _mixer_call(x2d, states, params, *, n_streams, t_len, streams_per_step, tt, pairwise):
    stateful = states is not None
    if stateful:
        L = t_len
        assert L % SUBLANE == 0 and L <= CHUNK
        n_chunks = streams_per_step
        tt = streams_per_step * L
        n_tblocks = 1
        grid = (n_streams // streams_per_step, 1)
        ns = streams_per_step
    else:
        L = CHUNK
        assert t_len % tt == 0 and tt % L == 0
        n_chunks = tt // L
        n_tblocks = t_len // tt
        grid = (n_streams, n_tblocks)
        ns = 1

    x_spec = pl.BlockSpec((tt, D_MODEL), lambda i, j: (i * n_tblocks + j, 0))
    state_shapes = [
        (n_streams, H_A, DK_A, DV_A), (n_streams, H_B, DK_B, DV_B), (n_streams, H_C, DK_C, DV_C),
        (n_streams, H_C, DK_C), (n_streams, SUBLANE, LANE), (n_streams, SUBLANE, W_QKC),
    ]

    def state_spec(shape):
        nd = len(shape)
        return pl.BlockSpec((ns,) + shape[1:], lambda i, j, _nd=nd: (i,) + (0,) * (_nd - 1))

    in_specs = [x_spec]
    args = [x2d]
    if stateful:
        in_specs += [state_spec(s) for s in state_shapes]
        args += list(states)
    in_specs += [_const_spec(p.shape) for p in params]
    args += list(params)

    dmin_shape = (grid[0] * grid[1], SUBLANE, LANE)
    dmin_spec = pl.BlockSpec((1, SUBLANE, LANE), lambda i, j: (i * n_tblocks + j, 0, 0))
    out_shape = ([jax.ShapeDtypeStruct(x2d.shape, F32)] + [jax.ShapeDtypeStruct(s, F32) for s in state_shapes]
                 + [jax.ShapeDtypeStruct(dmin_shape, F32)])
    out_specs = [x_spec] + [state_spec(s) for s in state_shapes] + [dmin_spec]

    scratch = [
        pltpu.VMEM((tt, D_MODEL), BF16),
        pltpu.VMEM((tt, W_LIN), F32),
        pltpu.VMEM((tt, W_LIN), F32),
        pltpu.VMEM((tt, W_LIN), F32),
        pltpu.VMEM((tt, W_QKC), F32),
        pltpu.VMEM((tt, W_C), BF16),
        pltpu.VMEM((tt, W_C), F32),
        pltpu.VMEM((tt, N_BRANCH * BRANCH_WIDTH), BF16),
        pltpu.VMEM((tt, W_LIN), BF16),
        pltpu.VMEM((tt, W_LIN), BF16),
        pltpu.VMEM((tt, W_LIN), BF16),
        pltpu.VMEM((tt, W_LIN), BF16),
        pltpu.VMEM((N_LIN, HP, HP), F32),
        pltpu.VMEM((H_C, DK_C, DV_C), F32),
        pltpu.VMEM((SUBLANE, LANE), F32),
        pltpu.VMEM((SUBLANE, LANE), F32),
        pltpu.VMEM((SUBLANE, W_QKC), F32),
        pltpu.VMEM((SUBLANE + (L if stateful else tt), W_QKC), F32),
    ]
    if pairwise:
        scratch += [
            pltpu.VMEM((n_chunks * N_LIN, L, L), BF16),
            pltpu.VMEM((tt, W_LIN), F32),
        ]
    kern = functools.partial(_mixer_kernel, L=L, n_chunks=n_chunks, n_tblocks=n_tblocks, stateful=stateful,
                             pairwise=pairwise)
    return pl.pallas_call(
        kern,
        grid=grid,
        in_specs=in_specs,
        out_specs=out_specs,
        out_shape=out_shape,
        scratch_shapes=scratch,
        compiler_params=pltpu.CompilerParams(
            dimension_semantics=("arbitrary", "arbitrary"), vmem_limit_bytes=VMEM_LIMIT_BYTES),
        name=("mixer_sample" if stateful else "mixer_prompt") + ("_pairwise" if pairwise else ""),
    )(*args)


def _ffn_kernel(*refs, n_sub, ls, n_tblocks, stateful, final):
    it = iter(refs)
    x_ref = next(it)
    if stateful:
        scf_in = next(it)
    g2_ref, wup_ref, convf_ref, wdown_ref, gfin_ref = (next(it) for _ in range(5))
    y_ref, scf_out = next(it), next(it)
    hist_s, cbuf_s, act_s = next(it), next(it), next(it)

    t_idx = pl.program_id(1)
    if not stateful:
        @pl.when(t_idx == 0)
        def _():
            hist_s[...] = jnp.zeros_like(hist_s)

    x = x_ref[...]
    hb = _rmsnorm_rows(x, g2_ref[...]).astype(BF16)
    ug = _dot(hb, wup_ref[:, :D_FF])
    uv = _dot(hb, wup_ref[:, D_FF:])

    wconv = convf_ref[...]
    for s in range(n_sub):
        r0 = s * ls
        cbuf_s[0:SUBLANE, :] = scf_in[s] if stateful else hist_s[...]
        cbuf_s[SUBLANE:SUBLANE + ls, :] = ug[r0:r0 + ls, :]
        new_hist = cbuf_s[ls:ls + SUBLANE, :]
        if stateful:
            scf_out[s] = new_hist
        else:
            hist_s[...] = new_hist
        acc = cbuf_s[CONV_F_ROW0:CONV_F_ROW0 + ls, :] * wconv[0:1, :]
        for j in range(1, CONV_F):
            acc = acc + cbuf_s[CONV_F_ROW0 + j:CONV_F_ROW0 + j + ls, :] * wconv[j:j + 1, :]
        gel = 0.5 * acc * (1.0 + lax.erf(acc * (2.0 ** -0.5)))
        act_s[r0:r0 + ls, :] = (gel * uv[r0:r0 + ls, :]).astype(BF16)

    x2 = x + _dot(act_s[...], wdown_ref[...])
    if final:
        x2 = _rmsnorm_rows(x2, gfin_ref[...])
    y_ref[...] = x2

    if not stateful:
        @pl.when(t_idx == n_tblocks - 1)
        def _():
            scf_out[0] = hist_s[...]


def _ffn_call(x2d, state, params, *, n_streams, t_len, streams_per_step, tt, final):
    stateful = state is not None
    if stateful:
        ls = t_len
        n_sub = streams_per_step
        tt = n_sub * ls
        n_tblocks = 1
        grid = (n_streams // n_sub, 1)
    else:
        assert t_len % tt == 0
        ls = tt
        n_sub = 1
        n_tblocks = t_len // tt
        grid = (n_streams, n_tblocks)

    x_spec = pl.BlockSpec((tt, D_MODEL), lambda i, j: (i * n_tblocks + j, 0))
    st_shape = (n_streams, SUBLANE, D_FF)
    st_spec = pl.BlockSpec((n_sub, SUBLANE, D_FF), lambda i, j: (i, 0, 0))
    in_specs = [x_spec]
    args = [x2d]
    if stateful:
        in_specs.append(st_spec)
        args.append(state)
    in_specs += [_const_spec(p.shape) for p in params]
    args += list(params)
    scratch = [
        pltpu.VMEM((SUBLANE, D_FF), F32),
        pltpu.VMEM((SUBLANE + ls, D_FF), F32),
        pltpu.VMEM((tt, D_FF), BF16),
    ]
    kern = functools.partial(_ffn_kernel, n_sub=n_sub, ls=ls, n_tblocks=n_tblocks, stateful=stateful, final=final)
    return pl.pallas_call(
        kern,
        grid=grid,
        in_specs=in_specs,
        out_specs=[x_spec, st_spec],
        out_shape=[jax.ShapeDtypeStruct(x2d.shape, F32), jax.ShapeDtypeStruct(st_shape, F32)],
        scratch_shapes=scratch,
        compiler_params=pltpu.CompilerParams(
            dimension_semantics=("arbitrary", "arbitrary"), vmem_limit_bytes=VMEM_LIMIT_BYTES),
        name="ffn_sample" if stateful else "ffn_prompt",
    )(*args)


def _pad_heads(a):
    r = a.shape[0]
    a = a.reshape(r, H_A, DK_A)
    return jnp.pad(a, ((0, 0), (0, 0), (0, HP - DK_A))).reshape(r, H_A * HP)


def _pack_cols(w):
    qa, ka, va, ga, ra, qb, fb, ib, gb, qkc, vc, oc, ic, fc, gm = jnp.split(w, SPLIT_IDX, axis=-1)
    ra_p = jnp.pad(ra, ((0, 0), (0, LANE - GLA_RANK)))
    if_p = jnp.pad(jnp.concatenate([ic, fc], axis=-1), ((0, 0), (0, LANE - 2 * H_C)))
    return jnp.concatenate(
        [_pad_heads(qa), qb, _pad_heads(ka), fb, va, ib, ga, gb, ra_p, if_p, qkc, vc, oc, gm], axis=-1)


def _layer_params(l, lower, norm1_g, w_in, b_in, w_gla_gate, b_gla_gate, gn_gla, gn_hgrn, conv_mlstm,
                  b_mlstm_f, gn_mlstm, w_branch, w_out, norm2_g, w_up, conv_ffn, w_down, final_g):
    lb = lower[l][None, :]
    wg = jnp.pad(_pad_heads(w_gla_gate[l]), ((0, LANE - GLA_RANK), (0, 0))).astype(BF16)
    bfrow = jnp.pad(b_mlstm_f[l][None, :], ((0, 0), (H_C, LANE - 2 * H_C)))
    mixer = (
        norm1_g[l][None, :], _pack_cols(w_in[l]).astype(BF16), _pack_cols(b_in[l][None, :]),
        wg, _pad_heads(b_gla_gate[l][None, :]),
        jnp.concatenate([gn_gla[l], gn_hgrn[l]])[None, :],
        jnp.log(lb), jnp.log1p(-lb), 1.0 - lb,
        conv_mlstm[l], bfrow, gn_mlstm[l][None, :],
        w_branch[l].astype(BF16), w_out[l].astype(BF16),
    )
    ffn = (norm2_g[l][None, :], w_up[l].astype(BF16), conv_ffn[l], w_down[l].astype(BF16), final_g[None, :])
    return mixer, ffn


def _run_group(x, states, layer_params, *, streams_per_step, tt_mixer, tt_ffn):
    n_streams, t_len, _ = x.shape
    x2d = x.reshape(n_streams * t_len, D_MODEL)
    new_states = []
    for l in range(DEPTH):
        mixer_p, ffn_p = layer_params[l]
        if states is None:
            mix_in, ffn_in = None, None
        else:
            s_gla, s_hgrn, s_c, s_n, s_m, s_cc, s_cf = (s[l] for s in states)
            m_tile = jnp.broadcast_to(jnp.pad(s_m, ((0, 0), (H_C, LANE - 2 * H_C)))[:, None, :],
                                      (n_streams, SUBLANE, LANE))
            cc_tile = jnp.pad(s_cc, ((0, 0), (CONV_C_ROW0, 0), (0, 0)))
            mix_in = (s_gla, s_hgrn, s_c, s_n, m_tile, cc_tile)
            ffn_in = jnp.pad(s_cf, ((0, 0), (CONV_F_ROW0, 0), (0, 0)))
        mixer = functools.partial(_mixer_call, x2d, mix_in, mixer_p, n_streams=n_streams, t_len=t_len,
                                  streams_per_step=streams_per_step, tt=tt_mixer)
        *outs, dmin = mixer(pairwise=False)
        out_of_range = jnp.min(dmin) < -MAX_FACTORED_DECAY
        x2d, o_gla, o_hgrn, o_c, o_n, o_m, o_cc = lax.cond(
            out_of_range, lambda: tuple(mixer(pairwise=True)[:-1]), lambda: tuple(outs))
        x2d, o_cf = _ffn_call(
            x2d, ffn_in, ffn_p, n_streams=n_streams, t_len=t_len,
            streams_per_step=streams_per_step, tt=tt_ffn, final=(l == DEPTH - 1))
        new_states.append((o_gla, o_hgrn, o_c, o_n, o_m[:, 0, H_C:2 * H_C], o_cc[:, CONV_C_ROW0:, :],
                           o_cf[:, CONV_F_ROW0:, :]))
    stacked = [jnp.stack([ns[i] for ns in new_states], axis=0) for i in range(7)]
    return x2d.reshape(n_streams, t_len, D_MODEL), stacked


def kernel(x_prompt, x_sample, state_gla, state_hgrn, state_mlstm_c, state_mlstm_n, state_mlstm_m,
           state_mlstm_conv, state_ffn_conv, norm1_g, w_in, b_in, w_gla_gate, b_gla_gate, gn_gla, lb_logits,
           gn_hgrn, conv_mlstm, b_mlstm_f, gn_mlstm, w_branch, w_out, norm2_g, w_up, conv_ffn, w_down, final_g):
    lb_cum = jnp.cumsum(jax.nn.softmax(lb_logits.astype(F32), axis=0), axis=0)
    lower = lb_cum - lb_cum[0]
    layer_params = [
        _layer_params(l, lower, norm1_g, w_in, b_in, w_gla_gate, b_gla_gate, gn_gla, gn_hgrn, conv_mlstm,
                      b_mlstm_f, gn_mlstm, w_branch, w_out, norm2_g, w_up, conv_ffn, w_down, final_g)
        for l in range(DEPTH)
    ]
    y_p, p_states = _run_group(x_prompt, None, layer_params, streams_per_step=1, tt_mixer=256, tt_ffn=512)
    sample_states = (state_gla, state_hgrn, state_mlstm_c, state_mlstm_n, state_mlstm_m,
                     state_mlstm_conv, state_ffn_conv)
    y_s, s_states = _run_group(x_sample, sample_states, layer_params, streams_per_step=4,
                               tt_mixer=None, tt_ffn=None)
    return (y_p, y_s, *p_states, *s_states)
```

```python
import functools

import jax
import jax.numpy as jnp
from jax import lax
from jax.experimental import pallas as pl
from jax.experimental.pallas import tpu as pltpu

F32 = jnp.float32
BF16 = jnp.bfloat16

D_MODEL = 1024
DEPTH = 2
CHUNK = 64
BRANCH_WIDTH = D_MODEL // 2
H_A, DV_A = 4, BRANCH_WIDTH // 4
DK_A = DV_A // 2
GLA_RANK = 16
GLA_TAU = 16.0
H_B = 4
DK_B = DV_B = BRANCH_WIDTH // H_B
H_C = 4
DK_C = DV_C = BRANCH_WIDTH // H_C
CONV_C = 4
N_BRANCH = 3
D_FF = ((8 * D_MODEL // 3 + 127) // 128) * 128
CONV_F = 3
EPS = 1e-6

SPLIT_SIZES = (
    H_A * DK_A, H_A * DK_A, H_A * DV_A, H_A * DV_A, GLA_RANK,
    H_B * DK_B, H_B * DK_B, H_B * DV_B, H_B * DV_B,
    2 * H_C * DK_C, H_C * DV_C, H_C * DV_C, H_C, H_C,
    N_BRANCH * D_MODEL,
)
SPLIT_IDX = tuple(int(sum(SPLIT_SIZES[:i + 1])) for i in range(len(SPLIT_SIZES) - 1))

LANE = 128
SUBLANE = 8
VMEM_LIMIT_BYTES = 60 * 1024 * 1024
MAX_FACTORED_DECAY = 80.0

N_LIN = H_A + H_B
W_KEY = H_A * DK_A + H_B * DK_B
W_LIN = N_LIN * LANE
O_Q = 0
O_K = O_Q + W_KEY
O_V = O_K + W_KEY
O_G = O_V + W_LIN
O_RA = O_G + W_LIN
O_IF = O_RA + LANE
O_QKC = O_IF + LANE
O_VC = O_QKC + 2 * H_C * DK_C
O_OC = O_VC + H_C * DV_C
O_GM = O_OC + H_C * DV_C
N_PACK = O_GM + N_BRANCH * D_MODEL
W_QKC = 2 * H_C * DK_C
W_C = H_C * DV_C
CONV_C_ROW0 = SUBLANE - (CONV_C - 1)
CONV_F_ROW0 = SUBLANE - (CONV_F - 1)


def _sigmoid(x):
    return 1.0 / (1.0 + jnp.exp(-x))


def _silu(x):
    return x * _sigmoid(x)


def _log_sigmoid(x):
    return jnp.minimum(x, 0.0) - jnp.log(1.0 + jnp.exp(-jnp.abs(x)))


def _rmsnorm_rows(x, g):
    return x * lax.rsqrt(jnp.mean(x * x, axis=-1, keepdims=True) + EPS) * g


def _dot(a, b):
    return jnp.dot(a, b, preferred_element_type=F32)


def _dot_nt(a, b):
    return lax.dot_general(a, b, (((1,), (1,)), ((), ())), preferred_element_type=F32)


def _dot_tn(a, b):
    return lax.dot_general(a, b, (((0,), (0,)), ((), ())), preferred_element_type=F32)


def _chunk_cumsum(a, chunk):
    rows, width = a.shape
    assert chunk % SUBLANE == 0 and rows % chunk == 0
    row_in_tile = lax.broadcasted_iota(jnp.int32, a.shape, 0) & (SUBLANE - 1)
    shift = 1
    while shift < SUBLANE:
        a = a + jnp.where(row_in_tile >= shift, pltpu.roll(a, shift, axis=0), 0.0)
        shift *= 2
    a3 = a.reshape(rows // chunk, chunk, width)
    tiles = [a3[:, 0:SUBLANE, :]]
    for i in range(1, chunk // SUBLANE):
        tiles.append(a3[:, i * SUBLANE:(i + 1) * SUBLANE, :] + tiles[-1][:, SUBLANE - 1:SUBLANE, :])
    return jnp.concatenate(tiles, axis=1).reshape(rows, width)


def _mixer_kernel(*refs, L, n_chunks, n_tblocks, stateful, pairwise):
    it = iter(refs)
    x_ref = next(it)
    if stateful:
        sgla_in, shgrn_in, sc_in, sn_in, sm_in, scc_in = (next(it) for _ in range(6))
    (g1_ref, win_ref, bin_ref, wg_ref, bg_ref, gnlin_ref, loglb_ref, log1mlb_ref, onemlb_ref,
     convm_ref, bf_ref, gnm_ref, wbr_ref, wout_ref) = (next(it) for _ in range(14))
    x1_ref, sgla_out, shgrn_out, sc_out, sn_out, sm_out, scc_out, dmin_out = (next(it) for _ in range(8))
    (h_s, k_s, la_s, g_s, qkp_s, vc_s, oc_s, o_s, qt_s, kt_s, kdec_s, vb_s,
     st_s, c_s, n_s, m_s, conv_s, cbuf_s) = (next(it) for _ in range(18))
    if pairwise:
        att_s, qf_s = next(it), next(it)

    t_idx = pl.program_id(1)
    tt = n_chunks * L
    last_c = n_chunks - 1
    wa = H_A * DK_A

    if not stateful:
        @pl.when(t_idx == 0)
        def _():
            st_s[...] = jnp.zeros_like(st_s)
            c_s[...] = jnp.zeros_like(c_s)
            n_s[...] = jnp.zeros_like(n_s)
            m_s[...] = jnp.zeros_like(m_s)
            conv_s[...] = jnp.zeros_like(conv_s)

    lane_k = lax.broadcasted_iota(jnp.int32, (L, LANE), 1)
    half_masks = [lane_k < DK_A, lane_k >= DK_A]

    def key_cols(h):
        if h < H_A:
            return slice((h // 2) * LANE, (h // 2 + 1) * LANE), half_masks[h % 2]
        return slice(wa + (h - H_A) * LANE, wa + (h - H_A + 1) * LANE), None

    def own_lanes(tile, mask):
        return tile if mask is None else jnp.where(mask, tile, jnp.zeros_like(tile))

    row_i = lax.broadcasted_iota(jnp.int32, (L, L), 0)
    col_i = lax.broadcasted_iota(jnp.int32, (L, L), 1)
    tril = row_i >= col_i

    x = x_ref[...]
    hb = _rmsnorm_rows(x, g1_ref[...]).astype(BF16)
    h_s[...] = hb

    def seg(off, width):
        return _dot(hb, win_ref[:, off:off + width]) + bin_ref[:, off:off + width]

    def lin_q():
        zq = seg(O_Q, W_KEY)
        return zq[:, :wa] * (DK_A ** -0.5), _silu(zq[:, wa:])

    z_small = seg(O_RA, 2 * LANE)
    ra = z_small[:, :LANE].astype(BF16)
    zif = z_small[:, LANE:]
    la_s[:, :wa] = _log_sigmoid(_dot(ra, wg_ref[...]) + bg_ref[...]) * (1.0 / GLA_TAU)
    zk = seg(O_K, W_KEY)
    k_s[:, :wa] = zk[:, :wa]
    fb = zk[:, wa:]
    u = jnp.exp(-jnp.abs(fb))
    log_sig = jnp.minimum(fb, 0.0) - jnp.log(1.0 + u)
    a_lb = loglb_ref[...]
    c_lb = log1mlb_ref[...] + log_sig
    la_s[:, wa:] = jnp.maximum(a_lb, c_lb) + jnp.log(1.0 + jnp.exp(-jnp.abs(a_lb - c_lb)))
    k_s[:, wa:] = onemlb_ref[...] * (jnp.where(fb >= 0.0, u, 1.0) / (1.0 + u))

    b = _chunk_cumsum(la_s[...], L)
    b3 = b.reshape(n_chunks, L, W_KEY)
    b_last = b3[:, L - 1:L, :]
    dec_last = jnp.exp(b_last)
    q_gla, q_hgrn = lin_q()
    eb = jnp.exp(b)
    qt_s[:, :wa] = (q_gla * eb[:, :wa]).astype(BF16)
    qt_s[:, wa:] = (q_hgrn * eb[:, wa:]).astype(BF16)
    kk = k_s[...]
    kt_s[...] = (kk * jnp.exp(-b)).astype(BF16)
    kdec_s[...] = (kk.reshape(n_chunks, L, W_KEY) * jnp.exp(b_last - b3)).reshape(tt, W_KEY).astype(BF16)
    vb_s[...] = seg(O_V, W_LIN).astype(BF16)
    g_s[...] = _silu(seg(O_G, W_LIN))
    qkp_s[...] = seg(O_QKC, W_QKC)
    vc_s[...] = seg(O_VC, W_C).astype(BF16)
    oc_s[...] = _sigmoid(seg(O_OC, W_C))
    lane = lax.broadcasted_iota(jnp.int32, zif.shape, 1)
    gates_all = jnp.where(lane >= H_C, _log_sigmoid(zif + bf_ref[...]), zif)
    gcum_all = _chunk_cumsum(gates_all, L)

    n_sub = n_chunks if stateful else 1
    ls = tt // n_sub
    wconv = convm_ref[...]
    for s in range(n_sub):
        r0 = s * ls
        cbuf_s[0:SUBLANE, :] = scc_in[s] if stateful else conv_s[...]
        cbuf_s[SUBLANE:SUBLANE + ls, :] = qkp_s[r0:r0 + ls, :]
        new_hist = cbuf_s[ls:ls + SUBLANE, :]
        if stateful:
            scc_out[s] = new_hist
        else:
            conv_s[...] = new_hist
        acc = cbuf_s[CONV_C_ROW0:CONV_C_ROW0 + ls, :] * wconv[0:1, :]
        for j in range(1, CONV_C):
            acc = acc + cbuf_s[CONV_C_ROW0 + j:CONV_C_ROW0 + j + ls, :] * wconv[j:j + 1, :]
        qk = _silu(acc)
        qkp_s[r0:r0 + ls, :W_C] = qk[:, :W_C]
        qkp_s[r0:r0 + ls, W_C:] = qk[:, W_C:] * (DK_C ** -0.5)

    dmin_out[...] = jnp.broadcast_to(jnp.min(b_last), dmin_out.shape)
    if pairwise:
        qf_s[:, :wa] = q_gla
        qf_s[:, wa:] = q_hgrn
        la_s[...] = b
        lane_l0 = lax.broadcasted_iota(jnp.int32, (L, LANE), 1)

        def per_chunk(c, carry):
            r0 = pl.multiple_of(c * L, L)
            for h in range(N_LIN):
                cols, mask = key_cols(h)
                qh = own_lanes(qf_s[pl.ds(r0, L), cols], mask)
                bh = la_s[pl.ds(r0, L), cols]

                def per_source_group(g, acc):
                    s0 = pl.multiple_of(g * SUBLANE, SUBLANE)
                    k8 = k_s[pl.ds(r0 + s0, SUBLANE), cols]
                    b8 = la_s[pl.ds(r0 + s0, SUBLANE), cols]
                    for j in range(SUBLANE):
                        e = jnp.exp(jnp.minimum(bh - b8[j:j + 1, :], 0.0))
                        col = jnp.sum(qh * k8[j:j + 1, :] * e, axis=-1, keepdims=True)
                        acc = jnp.where(lane_l0 == s0 + j, col, acc)
                    return acc

                acc = lax.fori_loop(0, L // SUBLANE, per_source_group, jnp.zeros((L, LANE), F32))
                att_s[c * N_LIN + h] = jnp.where(tril, acc[:, :L], 0.0).astype(BF16)
            return carry

        lax.fori_loop(0, n_chunks, per_chunk, 0)

    def lin_state_in(c, h):
        if not stateful:
            return st_s[h]
        if h < H_A:
            parts = [sgla_in[c, h], jnp.zeros((LANE - DK_A, DV_A), F32)]
            return jnp.concatenate(parts if h % 2 == 0 else parts[::-1], axis=0).T
        return shgrn_in[c, h - H_A].T

    def lin_state_out(c, h, s):
        if not stateful:
            st_s[h] = s
        elif h < H_A:
            sgla_out[c, h] = s.T[(h % 2) * DK_A:(h % 2 + 1) * DK_A, :]
        else:
            shgrn_out[c, h - H_A] = s.T

    gn = gnlin_ref[...]
    for h in range(N_LIN):
        cols = slice(h * LANE, (h + 1) * LANE)
        kcols, kmask = key_cols(h)
        qt_h = [own_lanes(qt_s[c * L:(c + 1) * L, kcols], kmask) for c in range(n_chunks)]
        att, upd, o_inter = [], [], []
        for c in range(n_chunks):
            rows = slice(c * L, (c + 1) * L)
            if pairwise:
                att.append(att_s[c * N_LIN + h])
            else:
                a = _dot_nt(qt_h[c], kt_s[rows, kcols])
                att.append(jnp.where(tril, a, 0.0).astype(BF16))
        for c in range(n_chunks):
            rows = slice(c * L, (c + 1) * L)
            upd.append(_dot_tn(vb_s[rows, cols], own_lanes(kdec_s[rows, kcols], kmask)))
        st = None
        for c in range(n_chunks):
            rows = slice(c * L, (c + 1) * L)
            if stateful or c == 0:
                st = lin_state_in(c, h)
            o_inter.append(_dot_nt(qt_h[c], st.astype(BF16)))
            st = st * dec_last[c][:, kcols] + upd[c]
            if stateful or c == last_c:
                lin_state_out(c, h, st)
        for c in range(n_chunks):
            rows = slice(c * L, (c + 1) * L)
            o = o_inter[c] + _dot(att[c], vb_s[rows, cols])
            o = o * lax.rsqrt(jnp.mean(o * o, axis=-1, keepdims=True) + EPS)
            o_s[rows, cols] = (o * gn[:, cols] * g_s[rows, cols]).astype(BF16)

    li_all = pltpu.roll(gates_all, H_C, axis=1)
    lane_l = lax.broadcasted_iota(jnp.int32, (L, LANE), 1)
    pad = LANE - L

    def transpose_rows(a):
        if pad:
            return jnp.concatenate([a, jnp.zeros((pad, LANE), F32)], axis=0).T[:, :L]
        return a.T

    head_cols = [slice(h * DK_C, (h + 1) * DK_C) for h in range(H_C)]
    head_kcols = [slice(W_C + h * DK_C, W_C + (h + 1) * DK_C) for h in range(H_C)]
    chunk_rows = [slice(c * L, (c + 1) * L) for c in range(n_chunks)]
    qkp = [[None] * H_C for _ in range(n_chunks)]
    upd_c = [[None] * H_C for _ in range(n_chunks)]
    upd_n = [[None] * H_C for _ in range(n_chunks)]
    o_intra = [[None] * H_C for _ in range(n_chunks)]
    dmax, rsum = [], []
    for c in range(n_chunks):
        rows = chunk_rows[c]
        gcum = gcum_all[rows, :]
        gcum_t = transpose_rows(gcum)
        li_t = transpose_rows(li_all[rows, :])
        dm_pack = jnp.zeros((L, LANE), F32)
        rs_pack = jnp.zeros((L, LANE), F32)
        for h in range(H_C):
            j = H_C + h
            dmat = jnp.where(tril, gcum[:, j:j + 1] - gcum_t[j:j + 1, :] + li_t[j:j + 1, :], -jnp.inf)
            dm = jnp.max(dmat, axis=1, keepdims=True)
            qh = qkp_s[rows, head_cols[h]].astype(BF16)
            kh = qkp_s[rows, head_kcols[h]].astype(BF16)
            p = _dot_nt(qh, kh) * jnp.exp(dmat - dm)
            qkp[c][h] = p.astype(BF16)
            dm_pack = jnp.where(lane_l == j, dm, dm_pack)
            rs_pack = jnp.where(lane_l == j, jnp.sum(p, axis=-1, keepdims=True), rs_pack)
        dmax.append(dm_pack)
        rsum.append(rs_pack)
    for c in range(n_chunks):
        rows = chunk_rows[c]
        gcum = gcum_all[rows, :]
        w_l = jnp.exp(gcum[L - 1:L, :] - gcum + li_all[rows, :] - dmax[c][L - 1:L, :])
        for h in range(H_C):
            j = H_C + h
            kw = qkp_s[rows, head_kcols[h]] * w_l[:, j:j + 1]
            vh = vc_s[rows, head_cols[h]]
            upd_c[c][h] = _dot_tn(kw.astype(BF16), vh)
            upd_n[c][h] = jnp.sum(kw, axis=0, keepdims=True)
            o_intra[c][h] = _dot(qkp[c][h], vh)
    gnm = gnm_ref[...]
    cst = [None] * H_C
    nrow = [None] * H_C
    m_row = None
    for c in range(n_chunks):
        rows = chunk_rows[c]
        if stateful:
            m_row = sm_in[c, 0:1, :]
        elif c == 0:
            m_row = m_s[0:1, :]
        qn_pack = jnp.zeros((L, LANE), F32)
        for h in range(H_C):
            if stateful:
                cst[h], nrow[h] = sc_in[c, h], sn_in[c, h:h + 1, :]
            elif c == 0:
                cst[h], nrow[h] = c_s[h], n_s[h:h + 1, :]
            qn = jnp.sum(qkp_s[rows, head_cols[h]] * nrow[h], axis=-1, keepdims=True)
            qn_pack = jnp.where(lane_l == H_C + h, qn, qn_pack)
        gcum = gcum_all[rows, :]
        w0 = gcum + m_row
        m_t = jnp.maximum(w0, dmax[c])
        s0 = jnp.exp(w0 - m_t)
        r = jnp.exp(dmax[c] - m_t)
        den = s0 * qn_pack + r * rsum[c]
        inv = 1.0 / jnp.maximum(jnp.abs(den), jnp.exp(-m_t))
        a_inter = s0 * inv
        a_intra = r * inv
        m_last = m_t[L - 1:L, :]
        s_l = jnp.exp(gcum[L - 1:L, :] + m_row - m_last)
        r_l = r[L - 1:L, :]
        m_row = m_last
        for h in range(H_C):
            j = H_C + h
            qh = qkp_s[rows, head_cols[h]].astype(BF16)
            hh = (a_inter[:, j:j + 1] * _dot(qh, cst[h].astype(BF16))
                  + a_intra[:, j:j + 1] * o_intra[c][h])
            cst[h] = s_l[:, j:j + 1] * cst[h] + r_l[:, j:j + 1] * upd_c[c][h]
            nrow[h] = s_l[:, j:j + 1] * nrow[h] + r_l[:, j:j + 1] * upd_n[c][h]
            if stateful:
                sc_out[c, h] = cst[h]
                sn_out[c, h:h + 1, :] = nrow[h]
            elif c == last_c:
                c_s[h] = cst[h]
                n_s[h:h + 1, :] = nrow[h]
            hh = hh * lax.rsqrt(jnp.mean(hh * hh, axis=-1, keepdims=True) + EPS)
            o_s[rows, W_LIN + h * DV_C:W_LIN + (h + 1) * DV_C] = (
                hh * gnm[:, head_cols[h]] * oc_s[rows, head_cols[h]]).astype(BF16)
        if stateful:
            sm_out[c] = jnp.broadcast_to(m_last, (SUBLANE, LANE))

    if not stateful:
        m_s[...] = jnp.broadcast_to(m_row, (SUBLANE, LANE))

    hb = h_s[...]
    merged = None
    for n in range(N_BRANCH):
        proj = _dot(o_s[:, n * BRANCH_WIDTH:(n + 1) * BRANCH_WIDTH], wbr_ref[n])
        gm = seg(O_GM + n * D_MODEL, D_MODEL)
        term = _sigmoid(gm) * proj
        merged = term if merged is None else merged + term
    x1_ref[...] = x_ref[...] + _dot(merged.astype(BF16), wout_ref[...])

    if not stateful:
        @pl.when(t_idx == n_tblocks - 1)
        def _():
            for h in range(H_A):
                sgla_out[0, h] = st_s[h].T[(h % 2) * DK_A:(h % 2 + 1) * DK_A, :]
            for h in range(H_B):
                shgrn_out[0, h] = st_s[H_A + h].T
            sc_out[0] = c_s[...]
            sn_out[0] = n_s[0:H_C, :]
            sm_out[0] = m_s[...]
            scc_out[0] = conv_s[...]


def _const_spec(shape):
    nd = len(shape)
    return pl.BlockSpec(shape, lambda i, j, _nd=nd: (0,) * _nd)


def _mixer_call(x2d, states, params, *, n_streams, t_len, streams_per_step, tt, pairwise):
    stateful = states is not None
    if stateful:
        L = t_len
        assert L % SUBLANE == 0 and L <= CHUNK
        n_chunks = streams_per_step
        tt = streams_per_step * L
        n_tblocks = 1
        grid = (n_streams // streams_per_step, 1)
        ns = streams_per_step
    else:
        L = CHUNK
        assert t_len % tt == 0 and tt % L == 0
        n_chunks = tt // L
        n_tblocks = t_len // tt
        grid = (n_streams, n_tblocks)
        ns = 1

    x_spec = pl.BlockSpec((tt, D_MODEL), lambda i, j: (i * n_tblocks + j, 0))
    state_shapes = [
        (n_streams, H_A, DK_A, DV_A), (n_streams, H_B, DK_B, DV_B), (n_streams, H_C, DK_C, DV_C),
        (n_streams, H_C, DK_C), (n_streams, SUBLANE, LANE), (n_streams, SUBLANE, W_QKC),
    ]

    def state_spec(shape):
        nd = len(shape)
        return pl.BlockSpec((ns,) + shape[1:], lambda i, j, _nd=nd: (i,) + (0,) * (_nd - 1))

    in_specs = [x_spec]
    args = [x2d]
    if stateful:
        in_specs += [state_spec(s) for s in state_shapes]
        args += list(states)
    in_specs += [_const_spec(p.shape) for p in params]
    args += list(params)

    dmin_shape = (grid[0] * grid[1], SUBLANE, LANE)
    dmin_spec = pl.BlockSpec((1, SUBLANE, LANE), lambda i, j: (i * n_tblocks + j, 0, 0))
    out_shape = ([jax.ShapeDtypeStruct(x2d.shape, F32)] + [jax.ShapeDtypeStruct(s, F32) for s in state_shapes]
                 + [jax.ShapeDtypeStruct(dmin_shape, F32)])
    out_specs = [x_spec] + [state_spec(s) for s in state_shapes] + [dmin_spec]

    scratch = [
        pltpu.VMEM((tt, D_MODEL), BF16),
        pltpu.VMEM((tt, W_KEY), F32),
        pltpu.VMEM((tt, W_KEY), F32),
        pltpu.VMEM((tt, W_LIN), F32),
        pltpu.VMEM((tt, W_QKC), F32),
        pltpu.VMEM((tt, W_C), BF16),
        pltpu.VMEM((tt, W_C), F32),
        pltpu.VMEM((tt, N_BRANCH * BRANCH_WIDTH), BF16),
        pltpu.VMEM((tt, W_KEY), BF16),
        pltpu.VMEM((tt, W_KEY), BF16),
        pltpu.VMEM((tt, W_KEY), BF16),
        pltpu.VMEM((tt, W_LIN), BF16),
        pltpu.VMEM((N_LIN, LANE, LANE), F32),
        pltpu.VMEM((H_C, DK_C, DV_C), F32),
        pltpu.VMEM((SUBLANE, LANE), F32),
        pltpu.VMEM((SUBLANE, LANE), F32),
        pltpu.VMEM((SUBLANE, W_QKC), F32),
        pltpu.VMEM((SUBLANE + (L if stateful else tt), W_QKC), F32),
    ]
    if pairwise:
        scratch += [
            pltpu.VMEM((n_chunks * N_LIN, L, L), BF16),
            pltpu.VMEM((tt, W_KEY), F32),
        ]
    kern = functools.partial(_mixer_kernel, L=L, n_chunks=n_chunks, n_tblocks=n_tblocks, stateful=stateful,
                             pairwise=pairwise)
    return pl.pallas_call(
        kern,
        grid=grid,
        in_specs=in_specs,
        out_specs=out_specs,
        out_shape=out_shape,
        scratch_shapes=scratch,
        compiler_params=pltpu.CompilerParams(
            dimension_semantics=("arbitrary", "arbitrary"), vmem_limit_bytes=VMEM_LIMIT_BYTES),
        name=("mixer_sample" if stateful else "mixer_prompt") + ("_pairwise" if pairwise else ""),
    )(*args)


def _ffn_kernel(*refs, n_sub, ls, n_tblocks, stateful, final):
    it = iter(refs)
    x_ref = next(it)
    if stateful:
        scf_in = next(it)
    g2_ref, wup_ref, convf_ref, wdown_ref, gfin_ref = (next(it) for _ in range(5))
    y_ref, scf_out = next(it), next(it)
    hist_s, cbuf_s, act_s = next(it), next(it), next(it)

    t_idx = pl.program_id(1)
    if not stateful:
        @pl.when(t_idx == 0)
        def _():
            hist_s[...] = jnp.zeros_like(hist_s)

    x = x_ref[...]
    hb = _rmsnorm_rows(x, g2_ref[...]).astype(BF16)
    ug = _dot(hb, wup_ref[:, :D_FF])
    uv = _dot(hb, wup_ref[:, D_FF:])

    wconv = convf_ref[...]
    for s in range(n_sub):
        r0 = s * ls
        cbuf_s[0:SUBLANE, :] = scf_in[s] if stateful else hist_s[...]
        cbuf_s[SUBLANE:SUBLANE + ls, :] = ug[r0:r0 + ls, :]
        new_hist = cbuf_s[ls:ls + SUBLANE, :]
        if stateful:
            scf_out[s] = new_hist
        else:
            hist_s[...] = new_hist
        acc = cbuf_s[CONV_F_ROW0:CONV_F_ROW0 + ls, :] * wconv[0:1, :]
        for j in range(1, CONV_F):
            acc = acc + cbuf_s[CONV_F_ROW0 + j:CONV_F_ROW0 + j + ls, :] * wconv[j:j + 1, :]
        gel = 0.5 * acc * (1.0 + lax.erf(acc * (2.0 ** -0.5)))
        act_s[r0:r0 + ls, :] = (gel * uv[r0:r0 + ls, :]).astype(BF16)

    x2 = x + _dot(act_s[...], wdown_ref[...])
    if final:
        x2 = _rmsnorm_rows(x2, gfin_ref[...])
    y_ref[...] = x2

    if not stateful:
        @pl.when(t_idx == n_tblocks - 1)
        def _():
            scf_out[0] = hist_s[...]


def _ffn_call(x2d, state, params, *, n_streams, t_len, streams_per_step, tt, final):
    stateful = state is not None
    if stateful:
        ls = t_len
        n_sub = streams_per_step
        tt = n_sub * ls
        n_tblocks = 1
        grid = (n_streams // n_sub, 1)
    else:
        assert t_len % tt == 0
        ls = tt
        n_sub = 1
        n_tblocks = t_len // tt
        grid = (n_streams, n_tblocks)

    x_spec = pl.BlockSpec((tt, D_MODEL), lambda i, j: (i * n_tblocks + j, 0))
    st_shape = (n_streams, SUBLANE, D_FF)
    st_spec = pl.BlockSpec((n_sub, SUBLANE, D_FF), lambda i, j: (i, 0, 0))
    in_specs = [x_spec]
    args = [x2d]
    if stateful:
        in_specs.append(st_spec)
        args.append(state)
    in_specs += [_const_spec(p.shape) for p in params]
    args += list(params)
    scratch = [
        pltpu.VMEM((SUBLANE, D_FF), F32),
        pltpu.VMEM((SUBLANE + ls, D_FF), F32),
        pltpu.VMEM((tt, D_FF), BF16),
    ]
    kern = functools.partial(_ffn_kernel, n_sub=n_sub, ls=ls, n_tblocks=n_tblocks, stateful=stateful, final=final)
    return pl.pallas_call(
        kern,
        grid=grid,
        in_specs=in_specs,
        out_specs=[x_spec, st_spec],
        out_shape=[jax.ShapeDtypeStruct(x2d.shape, F32), jax.ShapeDtypeStruct(st_shape, F32)],
        scratch_shapes=scratch,
        compiler_params=pltpu.CompilerParams(
            dimension_semantics=("arbitrary", "arbitrary"), vmem_limit_bytes=VMEM_LIMIT_BYTES),
        name="ffn_sample" if stateful else "ffn_prompt",
    )(*args)


def _pack_cols(w):
    qa, ka, va, ga, ra, qb, fb, ib, gb, qkc, vc, oc, ic, fc, gm = jnp.split(w, SPLIT_IDX, axis=-1)
    ra_p = jnp.pad(ra, ((0, 0), (0, LANE - GLA_RANK)))
    if_p = jnp.pad(jnp.concatenate([ic, fc], axis=-1), ((0, 0), (0, LANE - 2 * H_C)))
    return jnp.concatenate(
        [qa, qb, ka, fb, va, ib, ga, gb, ra_p, if_p, qkc, vc, oc, gm], axis=-1)


def _layer_params(l, lower, norm1_g, w_in, b_in, w_gla_gate, b_gla_gate, gn_gla, gn_hgrn, conv_mlstm,
                  b_mlstm_f, gn_mlstm, w_branch, w_out, norm2_g, w_up, conv_ffn, w_down, final_g):
    lb = lower[l][None, :]
    wg = jnp.pad(w_gla_gate[l], ((0, LANE - GLA_RANK), (0, 0))).astype(BF16)
    bfrow = jnp.pad(b_mlstm_f[l][None, :], ((0, 0), (H_C, LANE - 2 * H_C)))
    mixer = (
        norm1_g[l][None, :], _pack_cols(w_in[l]).astype(BF16), _pack_cols(b_in[l][None, :]),
        wg, b_gla_gate[l][None, :],
        jnp.concatenate([gn_gla[l], gn_hgrn[l]])[None, :],
        jnp.log(lb), jnp.log1p(-lb), 1.0 - lb,
        conv_mlstm[l], bfrow, gn_mlstm[l][None, :],
        w_branch[l].astype(BF16), w_out[l].astype(BF16),
    )
    ffn = (norm2_g[l][None, :], w_up[l].astype(BF16), conv_ffn[l], w_down[l].astype(BF16), final_g[None, :])
    return mixer, ffn


def _run_group(x, states, layer_params, *, streams_per_step, tt_mixer, tt_ffn):
    n_streams, t_len, _ = x.shape
    x2d = x.reshape(n_streams * t_len, D_MODEL)
    new_states = []
    for l in range(DEPTH):
        mixer_p, ffn_p = layer_params[l]
        if states is None:
            mix_in, ffn_in = None, None
        else:
            s_gla, s_hgrn, s_c, s_n, s_m, s_cc, s_cf = (s[l] for s in states)
            m_tile = jnp.broadcast_to(jnp.pad(s_m, ((0, 0), (H_C, LANE - 2 * H_C)))[:, None, :],
                                      (n_streams, SUBLANE, LANE))
            cc_tile = jnp.pad(s_cc, ((0, 0), (CONV_C_ROW0, 0), (0, 0)))
            mix_in = (s_gla, s_hgrn, s_c, s_n, m_tile, cc_tile)
            ffn_in = jnp.pad(s_cf, ((0, 0), (CONV_F_ROW0, 0), (0, 0)))
        mixer = functools.partial(_mixer_call, x2d, mix_in, mixer_p, n_streams=n_streams, t_len=t_len,
                                  streams_per_step=streams_per_step, tt=tt_mixer)
        *outs, dmin = mixer(pairwise=False)
        out_of_range = jnp.min(dmin) < -MAX_FACTORED_DECAY
        x2d, o_gla, o_hgrn, o_c, o_n, o_m, o_cc = lax.cond(
            out_of_range, lambda: tuple(mixer(pairwise=True)[:-1]), lambda: tuple(outs))
        x2d, o_cf = _ffn_call(
            x2d, ffn_in, ffn_p, n_streams=n_streams, t_len=t_len,
            streams_per_step=streams_per_step, tt=tt_ffn, final=(l == DEPTH - 1))
        new_states.append((o_gla, o_hgrn, o_c, o_n, o_m[:, 0, H_C:2 * H_C], o_cc[:, CONV_C_ROW0:, :],
                           o_cf[:, CONV_F_ROW0:, :]))
    stacked = [jnp.stack([ns[i] for ns in new_states], axis=0) for i in range(7)]
    return x2d.reshape(n_streams, t_len, D_MODEL), stacked


def kernel(x_prompt, x_sample, state_gla, state_hgrn, state_mlstm_c, state_mlstm_n, state_mlstm_m,
           state_mlstm_conv, state_ffn_conv, norm1_g, w_in, b_in, w_gla_gate, b_gla_gate, gn_gla, lb_logits,
           gn_hgrn, conv_mlstm, b_mlstm_f, gn_mlstm, w_branch, w_out, norm2_g, w_up, conv_ffn, w_down, final_g):
    lb_cum = jnp.cumsum(jax.nn.softmax(lb_logits.astype(F32), axis=0), axis=0)
    lower = lb_cum - lb_cum[0]
    layer_params = [
        _layer_params(l, lower, norm1_g, w_in, b_in, w_gla_gate, b_gla_gate, gn_gla, gn_hgrn, conv_mlstm,
                      b_mlstm_f, gn_mlstm, w_branch, w_out, norm2_g, w_up, conv_ffn, w_down, final_g)
        for l in range(DEPTH)
    ]
    y_p, p_states = _run_group(x_prompt, None, layer_params, streams_per_step=1, tt_mixer=512, tt_ffn=512)
    sample_states = (state_gla, state_hgrn, state_mlstm_c, state_mlstm_n, state_mlstm_m,
                     state_mlstm_conv, state_ffn_conv)
    y_s, s_states = _run_group(x_sample, sample_states, layer_params, streams_per_step=4,
                               tt_mixer=None, tt_ffn=None)
    return (y_p, y_s, *p_states, *s_states)
```

```python
import functools

import jax
import jax.numpy as jnp
from jax import lax
from jax.experimental import pallas as pl
from jax.experimental.pallas import tpu as pltpu

F32 = jnp.float32
BF16 = jnp.bfloat16

D_MODEL = 1024
DEPTH = 2
CHUNK = 64
BRANCH_WIDTH = D_MODEL // 2
H_A, DV_A = 4, BRANCH_WIDTH // 4
DK_A = DV_A // 2
GLA_RANK = 16
GLA_TAU = 16.0
H_B = 4
DK_B = DV_B = BRANCH_WIDTH // H_B
H_C = 4
DK_C = DV_C = BRANCH_WIDTH // H_C
CONV_C = 4
N_BRANCH = 3
D_FF = ((8 * D_MODEL // 3 + 127) // 128) * 128
CONV_F = 3
EPS = 1e-6

SPLIT_SIZES = (
    H_A * DK_A, H_A * DK_A, H_A * DV_A, H_A * DV_A, GLA_RANK,
    H_B * DK_B, H_B * DK_B, H_B * DV_B, H_B * DV_B,
    2 * H_C * DK_C, H_C * DV_C, H_C * DV_C, H_C, H_C,
    N_BRANCH * D_MODEL,
)
SPLIT_IDX = tuple(int(sum(SPLIT_SIZES[:i + 1])) for i in range(len(SPLIT_SIZES) - 1))

LANE = 128
SUBLANE = 8
VMEM_LIMIT_BYTES = 60 * 1024 * 1024
MAX_FACTORED_DECAY = 80.0

N_LIN = H_A + H_B
W_KEY = H_A * DK_A + H_B * DK_B
W_LIN = N_LIN * LANE
O_Q = 0
O_K = O_Q + W_KEY
O_V = O_K + W_KEY
O_G = O_V + W_LIN
O_RA = O_G + W_LIN
O_IF = O_RA + LANE
O_QKC = O_IF + LANE
O_VC = O_QKC + 2 * H_C * DK_C
O_OC = O_VC + H_C * DV_C
O_GM = O_OC + H_C * DV_C
N_PACK = O_GM + N_BRANCH * D_MODEL
W_QKC = 2 * H_C * DK_C
W_C = H_C * DV_C
LIN_HEAD_GROUP = 4
CONV_C_ROW0 = SUBLANE - (CONV_C - 1)
CONV_F_ROW0 = SUBLANE - (CONV_F - 1)


def _sigmoid(x):
    return 1.0 / (1.0 + jnp.exp(-x))


def _silu(x):
    return x * _sigmoid(x)


def _log_sigmoid(x):
    return jnp.minimum(x, 0.0) - jnp.log(1.0 + jnp.exp(-jnp.abs(x)))


def _rmsnorm_rows(x, g):
    return x * lax.rsqrt(jnp.mean(x * x, axis=-1, keepdims=True) + EPS) * g


def _dot(a, b):
    return jnp.dot(a, b, preferred_element_type=F32)


def _dot_nt(a, b):
    return lax.dot_general(a, b, (((1,), (1,)), ((), ())), preferred_element_type=F32)


def _dot_tn(a, b):
    return lax.dot_general(a, b, (((0,), (0,)), ((), ())), preferred_element_type=F32)


def _chunk_cumsum(a, chunk):
    rows, width = a.shape
    assert chunk % SUBLANE == 0 and rows % chunk == 0
    row_in_tile = lax.broadcasted_iota(jnp.int32, a.shape, 0) & (SUBLANE - 1)
    shift = 1
    while shift < SUBLANE:
        a = a + jnp.where(row_in_tile >= shift, pltpu.roll(a, shift, axis=0), 0.0)
        shift *= 2
    a3 = a.reshape(rows // chunk, chunk, width)
    tiles = [a3[:, 0:SUBLANE, :]]
    for i in range(1, chunk // SUBLANE):
        tiles.append(a3[:, i * SUBLANE:(i + 1) * SUBLANE, :] + tiles[-1][:, SUBLANE - 1:SUBLANE, :])
    return jnp.concatenate(tiles, axis=1).reshape(rows, width)


def _mixer_kernel(*refs, L, n_chunks, n_tblocks, stateful, pairwise):
    it = iter(refs)
    x_ref = next(it)
    if stateful:
        sgla_in, shgrn_in, sc_in, sn_in, sm_in, scc_in = (next(it) for _ in range(6))
    (g1_ref, win_ref, bin_ref, wg_ref, bg_ref, gnlin_ref, loglb_ref, log1mlb_ref, onemlb_ref,
     convm_ref, bf_ref, gnm_ref, wbr_ref, wout_ref) = (next(it) for _ in range(14))
    x1_ref, sgla_out, shgrn_out, sc_out, sn_out, sm_out, scc_out, dmin_out = (next(it) for _ in range(8))
    (h_s, k_s, la_s, g_s, qkp_s, vc_s, oc_s, o_s, qt_s, kt_s, kdec_s, vb_s,
     st_s, c_s, n_s, m_s, conv_s, cbuf_s) = (next(it) for _ in range(18))
    if pairwise:
        att_s, qf_s = next(it), next(it)

    t_idx = pl.program_id(1)
    tt = n_chunks * L
    last_c = n_chunks - 1
    wa = H_A * DK_A

    if not stateful:
        @pl.when(t_idx == 0)
        def _():
            st_s[...] = jnp.zeros_like(st_s)
            c_s[...] = jnp.zeros_like(c_s)
            n_s[...] = jnp.zeros_like(n_s)
            m_s[...] = jnp.zeros_like(m_s)
            conv_s[...] = jnp.zeros_like(conv_s)

    lane_k = lax.broadcasted_iota(jnp.int32, (L, LANE), 1)
    half_masks = [lane_k < DK_A, lane_k >= DK_A]

    def key_cols(h):
        if h < H_A:
            return slice((h // 2) * LANE, (h // 2 + 1) * LANE), half_masks[h % 2]
        return slice(wa + (h - H_A) * LANE, wa + (h - H_A + 1) * LANE), None

    def own_lanes(tile, mask):
        return tile if mask is None else jnp.where(mask, tile, jnp.zeros_like(tile))

    row_i = lax.broadcasted_iota(jnp.int32, (L, L), 0)
    col_i = lax.broadcasted_iota(jnp.int32, (L, L), 1)
    tril = row_i >= col_i

    x = x_ref[...]
    hb = _rmsnorm_rows(x, g1_ref[...]).astype(BF16)
    h_s[...] = hb

    def seg(off, width):
        return _dot(hb, win_ref[:, off:off + width]) + bin_ref[:, off:off + width]

    def lin_q():
        zq = seg(O_Q, W_KEY)
        return zq[:, :wa] * (DK_A ** -0.5), _silu(zq[:, wa:])

    z_small = seg(O_RA, 2 * LANE)
    ra = z_small[:, :LANE].astype(BF16)
    zif = z_small[:, LANE:]
    la_s[:, :wa] = _log_sigmoid(_dot(ra, wg_ref[...]) + bg_ref[...]) * (1.0 / GLA_TAU)
    zk = seg(O_K, W_KEY)
    vb_s[...] = seg(O_V, W_LIN).astype(BF16)
    k_s[:, :wa] = zk[:, :wa]
    fb = zk[:, wa:]
    u = jnp.exp(-jnp.abs(fb))
    log_sig = jnp.minimum(fb, 0.0) - jnp.log(1.0 + u)
    a_lb = loglb_ref[...]
    c_lb = log1mlb_ref[...] + log_sig
    la_s[:, wa:] = jnp.maximum(a_lb, c_lb) + jnp.log(1.0 + jnp.exp(-jnp.abs(a_lb - c_lb)))
    k_s[:, wa:] = onemlb_ref[...] * (jnp.where(fb >= 0.0, u, 1.0) / (1.0 + u))

    qkp_s[...] = seg(O_QKC, W_QKC)
    b = _chunk_cumsum(la_s[...], L)
    b3 = b.reshape(n_chunks, L, W_KEY)
    b_last = b3[:, L - 1:L, :]
    dec_last = jnp.exp(b_last)
    q_gla, q_hgrn = lin_q()
    eb = jnp.exp(b)
    qt_s[:, :wa] = (q_gla * eb[:, :wa]).astype(BF16)
    qt_s[:, wa:] = (q_hgrn * eb[:, wa:]).astype(BF16)
    vc_s[...] = seg(O_VC, W_C).astype(BF16)
    kk = k_s[...]
    kt = kk * jnp.exp(-b)
    kt_s[...] = kt.astype(BF16)
    if pairwise:
        kdec = kk.reshape(n_chunks, L, W_KEY) * jnp.exp(b_last - b3)
    else:
        kdec = kt.reshape(n_chunks, L, W_KEY) * dec_last
    kdec_s[...] = kdec.reshape(tt, W_KEY).astype(BF16)
    n_sub = n_chunks if stateful else 1
    ls = tt // n_sub
    wconv = convm_ref[...]
    for s in range(n_sub):
        r0 = s * ls
        cbuf_s[0:SUBLANE, :] = scc_in[s] if stateful else conv_s[...]
        cbuf_s[SUBLANE:SUBLANE + ls, :] = qkp_s[r0:r0 + ls, :]
        new_hist = cbuf_s[ls:ls + SUBLANE, :]
        if stateful:
            scc_out[s] = new_hist
        else:
            conv_s[...] = new_hist
        acc = cbuf_s[CONV_C_ROW0:CONV_C_ROW0 + ls, :] * wconv[0:1, :]
        for j in range(1, CONV_C):
            acc = acc + cbuf_s[CONV_C_ROW0 + j:CONV_C_ROW0 + j + ls, :] * wconv[j:j + 1, :]
        qk = _silu(acc)
        qkp_s[r0:r0 + ls, :W_C] = qk[:, :W_C]
        qkp_s[r0:r0 + ls, W_C:] = qk[:, W_C:] * (DK_C ** -0.5)
    g_s[...] = _silu(seg(O_G, W_LIN))
    oc_s[...] = _sigmoid(seg(O_OC, W_C))
    lane = lax.broadcasted_iota(jnp.int32, zif.shape, 1)
    gates_all = jnp.where(lane >= H_C, _log_sigmoid(zif + bf_ref[...]), zif)
    gcum_all = _chunk_cumsum(gates_all, L)

    dmin_out[...] = jnp.broadcast_to(jnp.min(b_last), dmin_out.shape)
    if pairwise:
        qf_s[:, :wa] = q_gla
        qf_s[:, wa:] = q_hgrn
        la_s[...] = b
        lane_l0 = lax.broadcasted_iota(jnp.int32, (L, LANE), 1)

        def per_chunk(c, carry):
            r0 = pl.multiple_of(c * L, L)
            for h in range(N_LIN):
                cols, mask = key_cols(h)
                qh = own_lanes(qf_s[pl.ds(r0, L), cols], mask)
                bh = la_s[pl.ds(r0, L), cols]

                def per_source_group(g, acc):
                    s0 = pl.multiple_of(g * SUBLANE, SUBLANE)
                    k8 = k_s[pl.ds(r0 + s0, SUBLANE), cols]
                    b8 = la_s[pl.ds(r0 + s0, SUBLANE), cols]
                    for j in range(SUBLANE):
                        e = jnp.exp(jnp.minimum(bh - b8[j:j + 1, :], 0.0))
                        col = jnp.sum(qh * k8[j:j + 1, :] * e, axis=-1, keepdims=True)
                        acc = jnp.where(lane_l0 == s0 + j, col, acc)
                    return acc

                acc = lax.fori_loop(0, L // SUBLANE, per_source_group, jnp.zeros((L, LANE), F32))
                att_s[c * N_LIN + h] = jnp.where(tril, acc[:, :L], 0.0).astype(BF16)
            return carry

        lax.fori_loop(0, n_chunks, per_chunk, 0)

    def lin_state_in(c, h):
        if not stateful:
            return st_s[h]
        if h < H_A:
            parts = [sgla_in[c, h], jnp.zeros((LANE - DK_A, DV_A), F32)]
            return jnp.concatenate(parts if h % 2 == 0 else parts[::-1], axis=0).T
        return shgrn_in[c, h - H_A].T

    def lin_state_out(c, h, s):
        if not stateful:
            st_s[h] = s
        elif h < H_A:
            sgla_out[c, h] = s.T[(h % 2) * DK_A:(h % 2 + 1) * DK_A, :]
        else:
            shgrn_out[c, h - H_A] = s.T

    gn = gnlin_ref[...]
    lin_rows = [slice(c * L, (c + 1) * L) for c in range(n_chunks)]
    for h0 in range(0, N_LIN, LIN_HEAD_GROUP):
        heads = range(h0, h0 + LIN_HEAD_GROUP)
        vcols = {h: slice(h * LANE, (h + 1) * LANE) for h in heads}
        keys = {h: key_cols(h) for h in heads}
        qt_h = {h: [own_lanes(qt_s[r, keys[h][0]], keys[h][1]) for r in lin_rows] for h in heads}
        att = {h: [] for h in heads}
        upd = {h: [] for h in heads}
        o_inter = {h: [] for h in heads}
        for h in heads:
            for c in range(n_chunks):
                if pairwise:
                    att[h].append(att_s[c * N_LIN + h])
                else:
                    a = _dot_nt(qt_h[h][c], kt_s[lin_rows[c], keys[h][0]])
                    att[h].append(jnp.where(tril, a, 0.0).astype(BF16))
        for h in heads:
            for c in range(n_chunks):
                kd = own_lanes(kdec_s[lin_rows[c], keys[h][0]], keys[h][1])
                upd[h].append(_dot_tn(vb_s[lin_rows[c], vcols[h]], kd))
        st = {h: None for h in heads}
        for c in range(n_chunks):
            for h in heads:
                if stateful or c == 0:
                    st[h] = lin_state_in(c, h)
                o_inter[h].append(_dot_nt(qt_h[h][c], st[h].astype(BF16)))
                st[h] = st[h] * dec_last[c][:, keys[h][0]] + upd[h][c]
                if stateful or c == last_c:
                    lin_state_out(c, h, st[h])
        for h in heads:
            for c in range(n_chunks):
                o = o_inter[h][c] + _dot(att[h][c], vb_s[lin_rows[c], vcols[h]])
                o = o * lax.rsqrt(jnp.mean(o * o, axis=-1, keepdims=True) + EPS)
                o_s[lin_rows[c], vcols[h]] = (o * gn[:, vcols[h]] * g_s[lin_rows[c], vcols[h]]).astype(BF16)

    li_all = pltpu.roll(gates_all, H_C, axis=1)
    lane_l = lax.broadcasted_iota(jnp.int32, (L, LANE), 1)
    pad = LANE - L

    def transpose_rows(a):
        if pad:
            return jnp.concatenate([a, jnp.zeros((pad, LANE), F32)], axis=0).T[:, :L]
        return a.T

    head_cols = [slice(h * DK_C, (h + 1) * DK_C) for h in range(H_C)]
    head_kcols = [slice(W_C + h * DK_C, W_C + (h + 1) * DK_C) for h in range(H_C)]
    chunk_rows = [slice(c * L, (c + 1) * L) for c in range(n_chunks)]
    qkp = [[None] * H_C for _ in range(n_chunks)]
    upd_c = [[None] * H_C for _ in range(n_chunks)]
    upd_n = [[None] * H_C for _ in range(n_chunks)]
    o_intra = [[None] * H_C for _ in range(n_chunks)]
    dmax, rsum = [], []
    for c in range(n_chunks):
        rows = chunk_rows[c]
        gcum = gcum_all[rows, :]
        gcum_t = transpose_rows(gcum)
        li_t = transpose_rows(li_all[rows, :])
        dm_pack = jnp.zeros((L, LANE), F32)
        rs_pack = jnp.zeros((L, LANE), F32)
        for h in range(H_C):
            j = H_C + h
            dmat = jnp.where(tril, gcum[:, j:j + 1] - gcum_t[j:j + 1, :] + li_t[j:j + 1, :], -jnp.inf)
            dm = jnp.max(dmat, axis=1, keepdims=True)
            qh = qkp_s[rows, head_cols[h]].astype(BF16)
            kh = qkp_s[rows, head_kcols[h]].astype(BF16)
            p = _dot_nt(qh, kh) * jnp.exp(dmat - dm)
            qkp[c][h] = p.astype(BF16)
            dm_pack = jnp.where(lane_l == j, dm, dm_pack)
            rs_pack = jnp.where(lane_l == j, jnp.sum(p, axis=-1, keepdims=True), rs_pack)
        dmax.append(dm_pack)
        rsum.append(rs_pack)
    for c in range(n_chunks):
        rows = chunk_rows[c]
        gcum = gcum_all[rows, :]
        w_l = jnp.exp(gcum[L - 1:L, :] - gcum + li_all[rows, :] - dmax[c][L - 1:L, :])
        for h in range(H_C):
            j = H_C + h
            kw = qkp_s[rows, head_kcols[h]] * w_l[:, j:j + 1]
            vh = vc_s[rows, head_cols[h]]
            upd_c[c][h] = _dot_tn(kw.astype(BF16), vh)
            upd_n[c][h] = jnp.sum(kw, axis=0, keepdims=True)
            o_intra[c][h] = _dot(qkp[c][h], vh)
    gnm = gnm_ref[...]
    cst = [None] * H_C
    nrow = [None] * H_C
    m_row = None
    for c in range(n_chunks):
        rows = chunk_rows[c]
        if stateful:
            m_row = sm_in[c, 0:1, :]
        elif c == 0:
            m_row = m_s[0:1, :]
        qn_pack = jnp.zeros((L, LANE), F32)
        for h in range(H_C):
            if stateful:
                cst[h], nrow[h] = sc_in[c, h], sn_in[c, h:h + 1, :]
            elif c == 0:
                cst[h], nrow[h] = c_s[h], n_s[h:h + 1, :]
            qn = jnp.sum(qkp_s[rows, head_cols[h]] * nrow[h], axis=-1, keepdims=True)
            qn_pack = jnp.where(lane_l == H_C + h, qn, qn_pack)
        gcum = gcum_all[rows, :]
        w0 = gcum + m_row
        m_t = jnp.maximum(w0, dmax[c])
        s0 = jnp.exp(w0 - m_t)
        r = jnp.exp(dmax[c] - m_t)
        den = s0 * qn_pack + r * rsum[c]
        inv = 1.0 / jnp.maximum(jnp.abs(den), jnp.exp(-m_t))
        a_inter = s0 * inv
        a_intra = r * inv
        m_last = m_t[L - 1:L, :]
        s_l = jnp.exp(gcum[L - 1:L, :] + m_row - m_last)
        r_l = r[L - 1:L, :]
        m_row = m_last
        for h in range(H_C):
            j = H_C + h
            qh = qkp_s[rows, head_cols[h]].astype(BF16)
            hh = (a_inter[:, j:j + 1] * _dot(qh, cst[h].astype(BF16))
                  + a_intra[:, j:j + 1] * o_intra[c][h])
            cst[h] = s_l[:, j:j + 1] * cst[h] + r_l[:, j:j + 1] * upd_c[c][h]
            nrow[h] = s_l[:, j:j + 1] * nrow[h] + r_l[:, j:j + 1] * upd_n[c][h]
            if stateful:
                sc_out[c, h] = cst[h]
                sn_out[c, h:h + 1, :] = nrow[h]
            elif c == last_c:
                c_s[h] = cst[h]
                n_s[h:h + 1, :] = nrow[h]
            hh = hh * lax.rsqrt(jnp.mean(hh * hh, axis=-1, keepdims=True) + EPS)
            o_s[rows, W_LIN + h * DV_C:W_LIN + (h + 1) * DV_C] = (
                hh * gnm[:, head_cols[h]] * oc_s[rows, head_cols[h]]).astype(BF16)
        if stateful:
            sm_out[c] = jnp.broadcast_to(m_last, (SUBLANE, LANE))

    if not stateful:
        m_s[...] = jnp.broadcast_to(m_row, (SUBLANE, LANE))

    hb = h_s[...]
    merged = None
    for n in range(N_BRANCH):
        proj = _dot(o_s[:, n * BRANCH_WIDTH:(n + 1) * BRANCH_WIDTH], wbr_ref[n])
        gm = seg(O_GM + n * D_MODEL, D_MODEL)
        term = _sigmoid(gm) * proj
        merged = term if merged is None else merged + term
    x1_ref[...] = x_ref[...] + _dot(merged.astype(BF16), wout_ref[...])

    if not stateful:
        @pl.when(t_idx == n_tblocks - 1)
        def _():
            for h in range(H_A):
                sgla_out[0, h] = st_s[h].T[(h % 2) * DK_A:(h % 2 + 1) * DK_A, :]
            for h in range(H_B):
                shgrn_out[0, h] = st_s[H_A + h].T
            sc_out[0] = c_s[...]
            sn_out[0] = n_s[0:H_C, :]
            sm_out[0] = m_s[...]
            scc_out[0] = conv_s[...]


def _const_spec(shape):
    nd = len(shape)
    return pl.BlockSpec(shape, lambda i, j, _nd=nd: (0,) * _nd)


def _mixer_call(x2d, states, params, *, n_streams, t_len, streams_per_step, tt, pairwise):
    stateful = states is not None
    if stateful:
        L = t_len
        assert L % SUBLANE == 0 and L <= CHUNK
        n_chunks = streams_per_step
        tt = streams_per_step * L
        n_tblocks = 1
        grid = (n_streams // streams_per_step, 1)
        ns = streams_per_step
    else:
        L = CHUNK
        assert t_len % tt == 0 and tt % L == 0
        n_chunks = tt // L
        n_tblocks = t_len // tt
        grid = (n_streams, n_tblocks)
        ns = 1

    x_spec = pl.BlockSpec((tt, D_MODEL), lambda i, j: (i * n_tblocks + j, 0))
    state_shapes = [
        (n_streams, H_A, DK_A, DV_A), (n_streams, H_B, DK_B, DV_B), (n_streams, H_C, DK_C, DV_C),
        (n_streams, H_C, DK_C), (n_streams, SUBLANE, LANE), (n_streams, SUBLANE, W_QKC),
    ]

    def state_spec(shape):
        nd = len(shape)
        return pl.BlockSpec((ns,) + shape[1:], lambda i, j, _nd=nd: (i,) + (0,) * (_nd - 1))

    in_specs = [x_spec]
    args = [x2d]
    if stateful:
        in_specs += [state_spec(s) for s in state_shapes]
        args += list(states)
    in_specs += [_const_spec(p.shape) for p in params]
    args += list(params)

    dmin_shape = (grid[0] * grid[1], SUBLANE, LANE)
    dmin_spec = pl.BlockSpec((1, SUBLANE, LANE), lambda i, j: (i * n_tblocks + j, 0, 0))
    out_shape = ([jax.ShapeDtypeStruct(x2d.shape, F32)] + [jax.ShapeDtypeStruct(s, F32) for s in state_shapes]
                 + [jax.ShapeDtypeStruct(dmin_shape, F32)])
    out_specs = [x_spec] + [state_spec(s) for s in state_shapes] + [dmin_spec]

    scratch = [
        pltpu.VMEM((tt, D_MODEL), BF16),
        pltpu.VMEM((tt, W_KEY), F32),
        pltpu.VMEM((tt, W_KEY), F32),
        pltpu.VMEM((tt, W_LIN), F32),
        pltpu.VMEM((tt, W_QKC), F32),
        pltpu.VMEM((tt, W_C), BF16),
        pltpu.VMEM((tt, W_C), F32),
        pltpu.VMEM((tt, N_BRANCH * BRANCH_WIDTH), BF16),
        pltpu.VMEM((tt, W_KEY), BF16),
        pltpu.VMEM((tt, W_KEY), BF16),
        pltpu.VMEM((tt, W_KEY), BF16),
        pltpu.VMEM((tt, W_LIN), BF16),
        pltpu.VMEM((N_LIN, LANE, LANE), F32),
        pltpu.VMEM((H_C, DK_C, DV_C), F32),
        pltpu.VMEM((SUBLANE, LANE), F32),
        pltpu.VMEM((SUBLANE, LANE), F32),
        pltpu.VMEM((SUBLANE, W_QKC), F32),
        pltpu.VMEM((SUBLANE + (L if stateful else tt), W_QKC), F32),
    ]
    if pairwise:
        scratch += [
            pltpu.VMEM((n_chunks * N_LIN, L, L), BF16),
            pltpu.VMEM((tt, W_KEY), F32),
        ]
    kern = functools.partial(_mixer_kernel, L=L, n_chunks=n_chunks, n_tblocks=n_tblocks, stateful=stateful,
                             pairwise=pairwise)
    return pl.pallas_call(
        kern,
        grid=grid,
        in_specs=in_specs,
        out_specs=out_specs,
        out_shape=out_shape,
        scratch_shapes=scratch,
        compiler_params=pltpu.CompilerParams(
            dimension_semantics=("arbitrary", "arbitrary"), vmem_limit_bytes=VMEM_LIMIT_BYTES),
        name=("mixer_sample" if stateful else "mixer_prompt") + ("_pairwise" if pairwise else ""),
    )(*args)


def _ffn_kernel(*refs, n_sub, ls, n_tblocks, stateful, final):
    it = iter(refs)
    x_ref = next(it)
    if stateful:
        scf_in = next(it)
    g2_ref, wup_ref, convf_ref, wdown_ref, gfin_ref = (next(it) for _ in range(5))
    y_ref, scf_out = next(it), next(it)
    hist_s, cbuf_s, act_s = next(it), next(it), next(it)

    t_idx = pl.program_id(1)
    if not stateful:
        @pl.when(t_idx == 0)
        def _():
            hist_s[...] = jnp.zeros_like(hist_s)

    x = x_ref[...]
    hb = _rmsnorm_rows(x, g2_ref[...]).astype(BF16)
    ug = _dot(hb, wup_ref[:, :D_FF])
    uv = _dot(hb, wup_ref[:, D_FF:])

    wconv = convf_ref[...]
    for s in range(n_sub):
        r0 = s * ls
        cbuf_s[0:SUBLANE, :] = scf_in[s] if stateful else hist_s[...]
        cbuf_s[SUBLANE:SUBLANE + ls, :] = ug[r0:r0 + ls, :]
        new_hist = cbuf_s[ls:ls + SUBLANE, :]
        if stateful:
            scf_out[s] = new_hist
        else:
            hist_s[...] = new_hist
        acc = cbuf_s[CONV_F_ROW0:CONV_F_ROW0 + ls, :] * wconv[0:1, :]
        for j in range(1, CONV_F):
            acc = acc + cbuf_s[CONV_F_ROW0 + j:CONV_F_ROW0 + j + ls, :] * wconv[j:j + 1, :]
        gel = 0.5 * acc * (1.0 + lax.erf(acc * (2.0 ** -0.5)))
        act_s[r0:r0 + ls, :] = (gel * uv[r0:r0 + ls, :]).astype(BF16)

    x2 = x + _dot(act_s[...], wdown_ref[...])
    if final:
        x2 = _rmsnorm_rows(x2, gfin_ref[...])
    y_ref[...] = x2

    if not stateful:
        @pl.when(t_idx == n_tblocks - 1)
        def _():
            scf_out[0] = hist_s[...]


def _ffn_call(x2d, state, params, *, n_streams, t_len, streams_per_step, tt, final):
    stateful = state is not None
    if stateful:
        ls = t_len
        n_sub = streams_per_step
        tt = n_sub * ls
        n_tblocks = 1
        grid = (n_streams // n_sub, 1)
    else:
        assert t_len % tt == 0
        ls = tt
        n_sub = 1
        n_tblocks = t_len // tt
        grid = (n_streams, n_tblocks)

    x_spec = pl.BlockSpec((tt, D_MODEL), lambda i, j: (i * n_tblocks + j, 0))
    st_shape = (n_streams, SUBLANE, D_FF)
    st_spec = pl.BlockSpec((n_sub, SUBLANE, D_FF), lambda i, j: (i, 0, 0))
    in_specs = [x_spec]
    args = [x2d]
    if stateful:
        in_specs.append(st_spec)
        args.append(state)
    in_specs += [_const_spec(p.shape) for p in params]
    args += list(params)
    scratch = [
        pltpu.VMEM((SUBLANE, D_FF), F32),
        pltpu.VMEM((SUBLANE + ls, D_FF), F32),
        pltpu.VMEM((tt, D_FF), BF16),
    ]
    kern = functools.partial(_ffn_kernel, n_sub=n_sub, ls=ls, n_tblocks=n_tblocks, stateful=stateful, final=final)
    return pl.pallas_call(
        kern,
        grid=grid,
        in_specs=in_specs,
        out_specs=[x_spec, st_spec],
        out_shape=[jax.ShapeDtypeStruct(x2d.shape, F32), jax.ShapeDtypeStruct(st_shape, F32)],
        scratch_shapes=scratch,
        compiler_params=pltpu.CompilerParams(
            dimension_semantics=("arbitrary", "arbitrary"), vmem_limit_bytes=VMEM_LIMIT_BYTES),
        name="ffn_sample" if stateful else "ffn_prompt",
    )(*args)


def _pack_cols(w):
    qa, ka, va, ga, ra, qb, fb, ib, gb, qkc, vc, oc, ic, fc, gm = jnp.split(w, SPLIT_IDX, axis=-1)
    ra_p = jnp.pad(ra, ((0, 0), (0, LANE - GLA_RANK)))
    if_p = jnp.pad(jnp.concatenate([ic, fc], axis=-1), ((0, 0), (0, LANE - 2 * H_C)))
    return jnp.concatenate(
        [qa, qb, ka, fb, va, ib, ga, gb, ra_p, if_p, qkc, vc, oc, gm], axis=-1)


def _layer_params(l, lower, norm1_g, w_in, b_in, w_gla_gate, b_gla_gate, gn_gla, gn_hgrn, conv_mlstm,
                  b_mlstm_f, gn_mlstm, w_branch, w_out, norm2_g, w_up, conv_ffn, w_down, final_g):
    lb = lower[l][None, :]
    wg = jnp.pad(w_gla_gate[l], ((0, LANE - GLA_RANK), (0, 0))).astype(BF16)
    bfrow = jnp.pad(b_mlstm_f[l][None, :], ((0, 0), (H_C, LANE - 2 * H_C)))
    mixer = (
        norm1_g[l][None, :], _pack_cols(w_in[l].astype(BF16)), _pack_cols(b_in[l][None, :]),
        wg, b_gla_gate[l][None, :],
        jnp.concatenate([gn_gla[l], gn_hgrn[l]])[None, :],
        jnp.log(lb), jnp.log1p(-lb), 1.0 - lb,
        conv_mlstm[l], bfrow, gn_mlstm[l][None, :],
        w_branch[l].astype(BF16), w_out[l].astype(BF16),
    )
    ffn = (norm2_g[l][None, :], w_up[l].astype(BF16), conv_ffn[l], w_down[l].astype(BF16), final_g[None, :])
    return mixer, ffn


def _run_group(x, states, layer_params, *, streams_per_step, tt_mixer, tt_ffn):
    n_streams, t_len, _ = x.shape
    x2d = x.reshape(n_streams * t_len, D_MODEL)
    new_states = []
    for l in range(DEPTH):
        mixer_p, ffn_p = layer_params[l]
        if states is None:
            mix_in, ffn_in = None, None
        else:
            s_gla, s_hgrn, s_c, s_n, s_m, s_cc, s_cf = (s[l] for s in states)
            m_tile = jnp.broadcast_to(jnp.pad(s_m, ((0, 0), (H_C, LANE - 2 * H_C)))[:, None, :],
                                      (n_streams, SUBLANE, LANE))
            cc_tile = jnp.pad(s_cc, ((0, 0), (CONV_C_ROW0, 0), (0, 0)))
            mix_in = (s_gla, s_hgrn, s_c, s_n, m_tile, cc_tile)
            ffn_in = jnp.pad(s_cf, ((0, 0), (CONV_F_ROW0, 0), (0, 0)))
        mixer = functools.partial(_mixer_call, x2d, mix_in, mixer_p, n_streams=n_streams, t_len=t_len,
                                  streams_per_step=streams_per_step, tt=tt_mixer)
        *outs, dmin = mixer(pairwise=False)
        out_of_range = jnp.min(dmin) < -MAX_FACTORED_DECAY
        x2d, o_gla, o_hgrn, o_c, o_n, o_m, o_cc = lax.cond(
            out_of_range, lambda: tuple(mixer(pairwise=True)[:-1]), lambda: tuple(outs))
        x2d, o_cf = _ffn_call(
            x2d, ffn_in, ffn_p, n_streams=n_streams, t_len=t_len,
            streams_per_step=streams_per_step, tt=tt_ffn, final=(l == DEPTH - 1))
        new_states.append((o_gla, o_hgrn, o_c, o_n, o_m[:, 0, H_C:2 * H_C], o_cc[:, CONV_C_ROW0:, :],
                           o_cf[:, CONV_F_ROW0:, :]))
    stacked = [jnp.stack([ns[i] for ns in new_states], axis=0) for i in range(7)]
    return x2d.reshape(n_streams, t_len, D_MODEL), stacked


def kernel(x_prompt, x_sample, state_gla, state_hgrn, state_mlstm_c, state_mlstm_n, state_mlstm_m,
           state_mlstm_conv, state_ffn_conv, norm1_g, w_in, b_in, w_gla_gate, b_gla_gate, gn_gla, lb_logits,
           gn_hgrn, conv_mlstm, b_mlstm_f, gn_mlstm, w_branch, w_out, norm2_g, w_up, conv_ffn, w_down, final_g):
    lb_cum = jnp.cumsum(jax.nn.softmax(lb_logits.astype(F32), axis=0), axis=0)
    lower = lb_cum - lb_cum[0]
    layer_params = [
        _layer_params(l, lower, norm1_g, w_in, b_in, w_gla_gate, b_gla_gate, gn_gla, gn_hgrn, conv_mlstm,
                      b_mlstm_f, gn_mlstm, w_branch, w_out, norm2_g, w_up, conv_ffn, w_down, final_g)
        for l in range(DEPTH)
    ]
    y_p, p_states = _run_group(x_prompt, None, layer_params, streams_per_step=1, tt_mixer=512, tt_ffn=1024)
    sample_states = (state_gla, state_hgrn, state_mlstm_c, state_mlstm_n, state_mlstm_m,
                     state_mlstm_conv, state_ffn_conv)
    y_s, s_states = _run_group(x_sample, sample_states, layer_params, streams_per_step=8,
                               tt_mixer=None, tt_ffn=None)
    return (y_p, y_s, *p_states, *s_states)
```
